```python
import jax, jax.numpy as jnp
from jax import lax
import numpy as np

D_MODEL = 1024
BATCH = 4
SEQ = 8192
DEPTH = 1

CONV_CH = D_MODEL
CONV_WIDTH = 31
HEAD_DIM = 64
N_Q_HEADS = 16
N_KV_HEADS = 2
GROUP = N_Q_HEADS // N_KV_HEADS
ATTN_W = N_Q_HEADS * HEAD_DIM
KV_W = N_KV_HEADS * HEAD_DIM
WINDOW = 128
BLOCK = 128
ROPE_THETA = 10000.0
PEER_HEADS = 8
N_KEYS = 128
N_EXPERTS = N_KEYS * N_KEYS
PEER_DKEY = 256
PEER_DHALF = PEER_DKEY // 2
PEER_TOPK = 16
PEER_CHUNK = 128
IN_COLS = 2 * CONV_CH + ATTN_W + 2 * KV_W + 2 * D_MODEL
EPS = 1e-6
NEG = -1e30

kernel_name = "hybrid_conformer_swa_sink_peer_block"


def rmsnorm(x, g):
    xf = x.astype(jnp.float32)
    y = xf * lax.rsqrt(jnp.mean(xf * xf, axis=-1, keepdims=True) + EPS)
    return (y * g.astype(jnp.float32)).astype(x.dtype)


def layernorm(x, g, b):
    xf = x.astype(jnp.float32)
    mu = jnp.mean(xf, axis=-1, keepdims=True)
    xc = xf - mu
    y = xc * lax.rsqrt(jnp.mean(xc * xc, axis=-1, keepdims=True) + EPS)
    return (y * g.astype(jnp.float32) + b.astype(jnp.float32)).astype(x.dtype)


def rope(t, positions):
    half = HEAD_DIM // 2
    inv = ROPE_THETA ** (-(jnp.arange(half, dtype=jnp.float32) * 2.0 / HEAD_DIM))
    ang = positions.astype(jnp.float32)[..., None] * inv
    cos = jnp.cos(ang)[:, :, None, :]
    sin = jnp.sin(ang)[:, :, None, :]
    tf = t.astype(jnp.float32)
    t1, t2 = tf[..., :half], tf[..., half:]
    out = jnp.concatenate([t1 * cos - t2 * sin, t2 * cos + t1 * sin], axis=-1)
    return out.astype(t.dtype)


def conformer_conv(val, gate, conv_w, conv_b, ln_g, ln_b, w_conv_out):
    u = val * jax.nn.sigmoid(gate)
    up = jnp.pad(u, ((0, 0), (CONV_WIDTH - 1, 0), (0, 0)))
    y = lax.conv_general_dilated(
        up, conv_w[:, None, :], window_strides=(1,), padding="VALID",
        dimension_numbers=("NWC", "WIO", "NWC"), feature_group_count=CONV_CH) + conv_b
    y = jax.nn.silu(layernorm(y, ln_g, ln_b))
    return y @ w_conv_out


def swa_sink_attention(q, k, v, sinks):
    B, S = q.shape[0], q.shape[1]
    nb = S // BLOCK
    qb = q.reshape(B, nb, BLOCK, N_KV_HEADS, GROUP, HEAD_DIM)

    def band(t):
        tp = jnp.pad(t, ((0, 0), (BLOCK, 0), (0, 0), (0, 0)))
        tb = tp.reshape(B, nb + 1, BLOCK, N_KV_HEADS, HEAD_DIM)
        return jnp.concatenate([tb[:, :-1], tb[:, 1:]], axis=2)

    kb, vb = band(k), band(v)
    s = jnp.einsum("bnqhgd,bnkhd->bnhgqk", qb, kb,
                   preferred_element_type=jnp.float32) * (HEAD_DIM ** -0.5)
    qi = jnp.arange(BLOCK)[:, None]
    kj = jnp.arange(2 * BLOCK)[None, :]
    rel = qi + BLOCK - kj
    in_window = (rel >= 0) & (rel < WINDOW)
    not_pad = (kj >= BLOCK)[None] | (jnp.arange(nb)[:, None, None] > 0)
    mask = in_window[None] & not_pad
    s = jnp.where(mask[None, :, None, None], s, NEG)
    sk = sinks.astype(jnp.float32).reshape(N_KV_HEADS, GROUP)[:, :, None, None]
    m = jnp.maximum(jnp.max(s, axis=-1, keepdims=True), sk)
    e = jnp.exp(s - m)
    p = e / (jnp.sum(e, axis=-1, keepdims=True) + jnp.exp(sk - m))
    o = jnp.einsum("bnhgqk,bnkhd->bnqhgd", p.astype(v.dtype), vb)
    return o.reshape(B, S, ATTN_W)


def peer(h, w_pq, sub_keys, u_emb, v_emb):
    B, S, D = h.shape
    q = (h @ w_pq).reshape(B, S, PEER_HEADS, 2, PEER_DHALF)
    sc = jnp.einsum("bshcd,hcnd->bshcn", q, sub_keys,
                    preferred_element_type=jnp.float32)
    s_top, i_top = lax.top_k(sc, PEER_TOPK)
    cand = s_top[..., 0, :, None] + s_top[..., 1, None, :]
    cand_idx = i_top[..., 0, :, None] * N_KEYS + i_top[..., 1, None, :]
    cand = cand.reshape(B, S, PEER_HEADS, PEER_TOPK * PEER_TOPK)
    cand_idx = cand_idx.reshape(B, S, PEER_HEADS, PEER_TOPK * PEER_TOPK)
    best, pos = lax.top_k(cand, PEER_TOPK)
    expert_idx = jnp.take_along_axis(cand_idx, pos, axis=-1)
    gate = jax.nn.softmax(best, axis=-1).astype(h.dtype)

    n_chunks = (B * S) // PEER_CHUNK
    hc = h.reshape(n_chunks, PEER_CHUNK, D)
    ic = expert_idx.reshape(n_chunks, PEER_CHUNK, PEER_HEADS * PEER_TOPK)
    gc = gate.reshape(n_chunks, PEER_CHUNK, PEER_HEADS * PEER_TOPK)

    def chunk(args):
        hx, idx, g = args
        a = jax.nn.gelu(jnp.einsum("ced,cd->ce", u_emb[idx], hx), approximate=False)
        return jnp.einsum("ce,ced->cd", g * a, v_emb[idx])

    out = lax.map(chunk, (hc, ic, gc))
    return out.reshape(B, S, D)


def setup_inputs(seed: int = 0) -> dict:
    key = jax.random.key(seed)
    ks = jax.random.split(key, 20)
    f32 = jnp.float32
    nrm = lambda k, shape, scale: jax.random.normal(k, shape, f32) * scale
    L = DEPTH
    return {
        "x": nrm(ks[0], (BATCH, SEQ, D_MODEL), 1.0),
        "positions": jnp.broadcast_to(jnp.arange(SEQ, dtype=jnp.int32)[None, :], (BATCH, SEQ)),
        "norm1_g": 1.0 + nrm(ks[1], (L, D_MODEL), 0.02),
        "w_in": nrm(ks[2], (L, D_MODEL, IN_COLS), D_MODEL ** -0.5),
        "b_in": nrm(ks[3], (L, IN_COLS), 0.01),
        "conv_w": nrm(ks[4], (L, CONV_WIDTH, CONV_CH), CONV_WIDTH ** -0.5),
        "conv_b": nrm(ks[5], (L, CONV_CH), 0.01),
        "conv_ln_g": 1.0 + nrm(ks[6], (L, CONV_CH), 0.02),
        "conv_ln_b": nrm(ks[7], (L, CONV_CH), 0.01),
        "w_conv_out": nrm(ks[8], (L, CONV_CH, D_MODEL), CONV_CH ** -0.5),
        "attn_sinks": nrm(ks[9], (L, N_Q_HEADS), 0.5),
        "w_attn_o": nrm(ks[10], (L, ATTN_W, D_MODEL), ATTN_W ** -0.5),
        "w_out": nrm(ks[11], (L, D_MODEL, D_MODEL), D_MODEL ** -0.5),
        "norm2_g": 1.0 + nrm(ks[12], (L, D_MODEL), 0.02),
        "w_peer_q": nrm(ks[13], (L, D_MODEL, PEER_HEADS * PEER_DKEY), D_MODEL ** -0.5),
        "peer_sub_keys": nrm(ks[14], (L, PEER_HEADS, 2, N_KEYS, PEER_DHALF), PEER_DHALF ** -0.5),
        "peer_u": nrm(ks[15], (L, N_EXPERTS, D_MODEL), D_MODEL ** -0.5),
        "peer_v": nrm(ks[16], (L, N_EXPERTS, D_MODEL), (PEER_HEADS * PEER_TOPK) ** -0.5),
        "final_g": 1.0 + nrm(ks[17], (D_MODEL,), 0.02),
    }


def reference(x, positions, norm1_g, w_in, b_in, conv_w, conv_b, conv_ln_g, conv_ln_b,
              w_conv_out, attn_sinks, w_attn_o, w_out, norm2_g, w_peer_q, peer_sub_keys,
              peer_u, peer_v, final_g):
    B, S, _ = x.shape
    c0 = CONV_CH
    c1 = c0 + CONV_CH
    c2 = c1 + ATTN_W
    c3 = c2 + KV_W
    c4 = c3 + KV_W
    c5 = c4 + D_MODEL
    for l in range(DEPTH):
        h = rmsnorm(x, norm1_g[l])
        z = h @ w_in[l] + b_in[l]
        glu_val, glu_gate, q, k, v, g_conv, g_attn = jnp.split(z, [c0, c1, c2, c3, c4, c5], axis=-1)

        conv_out = conformer_conv(glu_val, glu_gate, conv_w[l], conv_b[l],
                                  conv_ln_g[l], conv_ln_b[l], w_conv_out[l])

        q = rope(q.reshape(B, S, N_Q_HEADS, HEAD_DIM), positions)
        k = rope(k.reshape(B, S, N_KV_HEADS, HEAD_DIM), positions)
        v = v.reshape(B, S, N_KV_HEADS, HEAD_DIM)
        attn_out = swa_sink_attention(q, k, v, attn_sinks[l]) @ w_attn_o[l]

        merged = jax.nn.sigmoid(g_conv) * conv_out + jax.nn.sigmoid(g_attn) * attn_out
        x = x + merged @ w_out[l]

        h2 = rmsnorm(x, norm2_g[l])
        x = x + peer(h2, w_peer_q[l], peer_sub_keys[l], peer_u[l], peer_v[l])
    return rmsnorm(x, final_g)
```

```python
import functools

import jax
import jax.numpy as jnp
import numpy as np
from jax import lax
from jax.experimental import pallas as pl
from jax.experimental.pallas import tpu as pltpu

D_MODEL = 1024
CONV_CH = D_MODEL
CONV_WIDTH = 31
HEAD_DIM = 64
N_Q_HEADS = 16
N_KV_HEADS = 2
GROUP = N_Q_HEADS // N_KV_HEADS
ATTN_W = N_Q_HEADS * HEAD_DIM
KV_W = N_KV_HEADS * HEAD_DIM
WINDOW = 128
BLOCK = 128
ROPE_THETA = 10000.0
PEER_HEADS = 8
N_KEYS = 128
N_EXPERTS = N_KEYS * N_KEYS
PEER_DKEY = 256
PEER_DHALF = PEER_DKEY // 2
PEER_TOPK = 16
N_PICKS = PEER_HEADS * PEER_TOPK
EPS = 1e-6
NEG = -1e30

SUBLANES = 8
LANES = 128
HALF_EXPERTS = N_EXPERTS // 2
VMEM_LIMIT = 56 * 1024 * 1024

Z_COLS = 2 * CONV_CH + ATTN_W + 2 * D_MODEL + 2 * KV_W

TM_PROJ = 256
T_CONV = 512
HALO = 32
TM_MIX = 256
TB_PEER = 128
TM_NORM = 512


def _cparams(sem):
    return pltpu.CompilerParams(dimension_semantics=sem, vmem_limit_bytes=VMEM_LIMIT)


def _resident(shape):
    nd = len(shape)
    return pl.BlockSpec(shape, lambda *_: (0,) * nd, pipeline_mode=pl.Buffered(1))


def _in_proj_body(x_ref, g_ref, w_ref, b_ref, z_ref):
    x = x_ref[...]
    h = x * lax.rsqrt(jnp.mean(x * x, axis=-1, keepdims=True) + EPS) * g_ref[...]
    z_ref[...] = jnp.dot(h.astype(jnp.bfloat16), w_ref[...],
                         preferred_element_type=jnp.float32) + b_ref[...]


def _in_proj(x2, g1, w_in, b_in):
    n = x2.shape[0]
    return pl.pallas_call(
        _in_proj_body,
        grid=(n // TM_PROJ,),
        in_specs=[pl.BlockSpec((TM_PROJ, D_MODEL), lambda i: (i, 0)),
                  _resident((1, D_MODEL)),
                  _resident((D_MODEL, Z_COLS)),
                  _resident((1, Z_COLS))],
        out_specs=pl.BlockSpec((TM_PROJ, Z_COLS), lambda i: (i, 0)),
        out_shape=jax.ShapeDtypeStruct((n, Z_COLS), jnp.float32),
        compiler_params=_cparams(("parallel",)),
        name="in_proj",
    )(x2, g1, w_in, b_in)


def _conv_body(val_ref, gate_ref, gc_ref, cw_ref, cb_ref, lg_ref, lb_ref, wo_ref, out_ref, ext_ref):
    s = pl.program_id(1)
    t = val_ref.shape[0]

    @pl.when(s == 0)
    def _():
        ext_ref[0:HALO, :] = jnp.zeros((HALO, CONV_CH), jnp.float32)

    @pl.when(s != 0)
    def _():
        ext_ref[0:HALO, :] = ext_ref[t:t + HALO, :]

    ext_ref[HALO:HALO + t, :] = val_ref[...] * jax.nn.sigmoid(gate_ref[...])

    base = HALO - (CONV_WIDTH - 1)
    y = jnp.zeros((t, CONV_CH), jnp.float32) + cb_ref[...]
    for j in range(CONV_WIDTH):
        y = y + cw_ref[j:j + 1, :] * ext_ref[base + j:base + j + t, :]

    mu = jnp.mean(y, axis=-1, keepdims=True)
    yc = y - mu
    yn = yc * lax.rsqrt(jnp.mean(yc * yc, axis=-1, keepdims=True) + EPS)
    yn = yn * lg_ref[...] + lb_ref[...]
    act = yn * jax.nn.sigmoid(yn)
    co = jnp.dot(act.astype(jnp.bfloat16), wo_ref[...], preferred_element_type=jnp.float32)
    out_ref[...] = jax.nn.sigmoid(gc_ref[...]) * co


def _conv_branch(z, batch, seq, conv_w, conv_b, ln_g, ln_b, w_conv_out):
    n = z.shape[0]
    nt = seq // T_CONV
    row = lambda b, s: b * nt + s
    return pl.pallas_call(
        _conv_body,
        grid=(batch, nt),
        in_specs=[pl.BlockSpec((T_CONV, CONV_CH), lambda b, s: (row(b, s), 0)),
                  pl.BlockSpec((T_CONV, CONV_CH), lambda b, s: (row(b, s), 1)),
                  pl.BlockSpec((T_CONV, D_MODEL), lambda b, s: (row(b, s), 3)),
                  _resident((CONV_WIDTH, CONV_CH)),
                  _resident((1, CONV_CH)),
                  _resident((1, CONV_CH)),
                  _resident((1, CONV_CH)),
                  _resident((CONV_CH, D_MODEL))],
        out_specs=pl.BlockSpec((T_CONV, D_MODEL), lambda b, s: (row(b, s), 0)),
        out_shape=jax.ShapeDtypeStruct((n, D_MODEL), jnp.float32),
        scratch_shapes=[pltpu.VMEM((T_CONV + HALO, CONV_CH), jnp.float32)],
        compiler_params=_cparams(("arbitrary", "arbitrary")),
        name="conv_branch",
    )(z, z, z, conv_w, conv_b, ln_g, ln_b, w_conv_out)


def _rot_half(x, first_half):
    return jnp.where(first_half, pltpu.roll(x, LANES - HEAD_DIM // 2, 1), pltpu.roll(x, HEAD_DIM // 2, 1))


def _rope_inv_freq():
    half = HEAD_DIM // 2
    f = (np.arange(LANES) % half).astype(np.float32)
    return (np.float32(ROPE_THETA) ** (-(f * np.float32(2.0) / np.float32(HEAD_DIM)))).astype(np.float32)[None, :]


def _attn_body(sinks_ref, pos_ref, inv_ref, q_ref, k_ref, v_ref, ga_ref, wo_ref, out_ref, kp_ref, vp_ref):
    n = pl.program_id(1)
    half = HEAD_DIM // 2

    lane = lax.broadcasted_iota(jnp.int32, (1, LANES), 1)
    ang = pos_ref[...].astype(jnp.float32) * inv_ref[...]
    cos = jnp.cos(ang)
    first_half = (lane % HEAD_DIM) < half
    sin = jnp.where(first_half, -jnp.sin(ang), jnp.sin(ang))

    def rope(x):
        return x * cos + _rot_half(x, first_half) * sin

    k_cur = rope(k_ref[...])
    v_cur = v_ref[...]

    @pl.when(n == 0)
    def _():
        kp_ref[...] = jnp.zeros_like(kp_ref)
        vp_ref[...] = jnp.zeros_like(vp_ref)

    kcat = jnp.concatenate([kp_ref[...], k_cur], axis=0)
    vcat = jnp.concatenate([vp_ref[...], v_cur], axis=0)
    kp_ref[...] = k_cur
    vp_ref[...] = v_cur

    lo = lane < HEAD_DIM
    kvar, vvar = {}, {}
    for g in range(N_KV_HEADS):
        own = lo if g == 0 else jnp.logical_not(lo)
        k_own = jnp.where(own, kcat, 0.0)
        v_own = jnp.where(own, vcat, 0.0)
        k_sw = pltpu.roll(k_own, HEAD_DIM, 1)
        v_sw = pltpu.roll(v_own, HEAD_DIM, 1)
        kvar[g] = ((k_own, k_sw) if g == 0 else (k_sw, k_own))
        vvar[g] = ((v_own, v_sw) if g == 0 else (v_sw, v_own))
        kvar[g] = tuple(a.astype(jnp.bfloat16) for a in kvar[g])
        vvar[g] = tuple(a.astype(jnp.bfloat16) for a in vvar[g])

    qi = lax.broadcasted_iota(jnp.int32, (BLOCK, 2 * BLOCK), 0)
    kj = lax.broadcasted_iota(jnp.int32, (BLOCK, 2 * BLOCK), 1)
    rel = qi + BLOCK - kj
    mask = (rel >= 0) & (rel < WINDOW) & ((kj >= BLOCK) | (n > 0))

    chunks = []
    for c in range(ATTN_W // LANES):
        q_c = rope(q_ref[:, c * LANES:(c + 1) * LANES]).astype(jnp.bfloat16)
        o_c = jnp.zeros((BLOCK, LANES), jnp.float32)
        for pos in range(2):
            h = 2 * c + pos
            g = h // GROUP
            s = lax.dot_general(q_c, kvar[g][pos], (((1,), (1,)), ((), ())),
                                preferred_element_type=jnp.float32) * (HEAD_DIM ** -0.5)
            s = jnp.where(mask, s, NEG)
            sk = sinks_ref[h]
            m = jnp.maximum(jnp.max(s, axis=-1, keepdims=True), sk)
            e = jnp.exp(s - m)
            p = e / (jnp.sum(e, axis=-1, keepdims=True) + jnp.exp(sk - m))
            o_c = o_c + jnp.dot(p.astype(jnp.bfloat16), vvar[g][pos],
                                preferred_element_type=jnp.float32)
        chunks.append(o_c)
    attn = jnp.concatenate(chunks, axis=1).astype(jnp.bfloat16)
    ao = jnp.dot(attn, wo_ref[...], preferred_element_type=jnp.float32)
    out_ref[...] = jax.nn.sigmoid(ga_ref[...]) * ao


def _attn_branch(z, pos_col, batch, seq, sinks, w_attn_o):
    n = z.shape[0]
    nb = seq // BLOCK
    row = lambda b, s: b * nb + s
    kcol = (2 * CONV_CH + ATTN_W + 2 * D_MODEL) // KV_W
    return pl.pallas_call(
        _attn_body,
        grid=(batch, nb),
        in_specs=[pl.BlockSpec(memory_space=pltpu.SMEM),
                  pl.BlockSpec((BLOCK, 1), lambda b, s: (row(b, s), 0)),
                  _resident((1, LANES)),
                  pl.BlockSpec((BLOCK, ATTN_W), lambda b, s: (row(b, s), 2)),
                  pl.BlockSpec((BLOCK, KV_W), lambda b, s: (row(b, s), kcol)),
                  pl.BlockSpec((BLOCK, KV_W), lambda b, s: (row(b, s), kcol + 1)),
                  pl.BlockSpec((BLOCK, D_MODEL), lambda b, s: (row(b, s), 4)),
                  _resident((ATTN_W, D_MODEL))],
        out_specs=pl.BlockSpec((BLOCK, D_MODEL), lambda b, s: (row(b, s), 0)),
        out_shape=jax.ShapeDtypeStruct((n, D_MODEL), jnp.float32),
        scratch_shapes=[pltpu.VMEM((BLOCK, KV_W), jnp.float32),
                        pltpu.VMEM((BLOCK, KV_W), jnp.float32)],
        compiler_params=_cparams(("arbitrary", "arbitrary")),
        name="attn_branch",
    )(sinks, pos_col, jnp.asarray(_rope_inv_freq()), z, z, z, z, w_attn_o)


def _topk_rows(vals, k):
    r = vals.shape[0]
    rows = lax.broadcasted_iota(jnp.int32, vals.shape, 0)
    top_v, top_i = [], []
    for _ in range(k):
        m = jnp.max(vals, axis=0, keepdims=True)
        idx = jnp.min(jnp.where(vals == m, rows, r), axis=0, keepdims=True)
        top_v.append(m)
        top_i.append(idx)
        vals = jnp.where(rows == idx, -jnp.inf, vals)
    return jnp.concatenate(top_v, axis=0), jnp.concatenate(top_i, axis=0)


def _take_rows(table, idx):
    rows = lax.broadcasted_iota(jnp.int32, table.shape, 0)
    return jnp.sum(jnp.where(rows == idx, table, 0), axis=0, keepdims=True)


def _mix_topk_body(x_ref, cg_ref, ag_ref, wout_ref, g2_ref, wpq_ref, keys_ref,
                   x1_ref, h2_ref, eidx_ref, gate_ref):
    merged = (cg_ref[...] + ag_ref[...]).astype(jnp.bfloat16)
    x1 = x_ref[...] + jnp.dot(merged, wout_ref[...], preferred_element_type=jnp.float32)
    x1_ref[...] = x1
    h2 = x1 * lax.rsqrt(jnp.mean(x1 * x1, axis=-1, keepdims=True) + EPS) * g2_ref[...]
    h2_ref[...] = h2
    qp = jnp.dot(h2.astype(jnp.bfloat16), wpq_ref[...],
                 preferred_element_type=jnp.float32).astype(jnp.bfloat16)

    for h in range(PEER_HEADS):
        tops = []
        for c in range(2):
            gi = 2 * h + c
            q_g = qp[:, gi * PEER_DHALF:(gi + 1) * PEER_DHALF]
            sc = lax.dot_general(keys_ref[gi], q_g, (((1,), (1,)), ((), ())),
                                 preferred_element_type=jnp.float32)
            tops.append(_topk_rows(sc, PEER_TOPK))
        (s1, i1), (s2, i2) = tops
        cand = jnp.concatenate([s1[a:a + 1, :] + s2 for a in range(PEER_TOPK)], axis=0)
        best, pos = _topk_rows(cand, PEER_TOPK)
        e = jnp.exp(best - best[0:1, :])
        gate_ref[h * PEER_TOPK:(h + 1) * PEER_TOPK, :] = e / jnp.sum(e, axis=0, keepdims=True)
        picks = []
        for j in range(PEER_TOPK):
            p = pos[j:j + 1, :]
            ia = _take_rows(i1, p >> 4)
            ib = _take_rows(i2, p & (PEER_TOPK - 1))
            picks.append(ia * N_KEYS + ib)
        eidx_ref[h * PEER_TOPK:(h + 1) * PEER_TOPK, :] = jnp.concatenate(picks, axis=0)


def _mix_topk(x2, cg, ag, w_out, g2, w_pq, keys):
    n = x2.shape[0]
    tile = lambda: pl.BlockSpec((TM_MIX, D_MODEL), lambda i: (i, 0))
    colt = lambda: pl.BlockSpec((N_PICKS, TM_MIX), lambda i: (0, i))
    return pl.pallas_call(
        _mix_topk_body,
        grid=(n // TM_MIX,),
        in_specs=[tile(), tile(), tile(),
                  _resident((D_MODEL, D_MODEL)),
                  _resident((1, D_MODEL)),
                  _resident((D_MODEL, PEER_HEADS * PEER_DKEY)),
                  _resident((2 * PEER_HEADS, N_KEYS, PEER_DHALF))],
        out_specs=[tile(), tile(), colt(), colt()],
        out_shape=[jax.ShapeDtypeStruct((n, D_MODEL), jnp.float32),
                   jax.ShapeDtypeStruct((n, D_MODEL), jnp.float32),
                   jax.ShapeDtypeStruct((N_PICKS, n), jnp.int32),
                   jax.ShapeDtypeStruct((N_PICKS, n), jnp.float32)],
        compiler_params=_cparams(("parallel",)),
        name="mix_topk",
    )(x2, cg, ag, w_out, g2, w_pq, keys)


def _pack_table(tbl):
    bits = lax.bitcast_convert_type(tbl.astype(jnp.bfloat16), jnp.uint16).astype(jnp.uint32)
    packed = bits[:HALF_EXPERTS] | (bits[HALF_EXPERTS:] << 16)
    return packed.reshape(HALF_EXPERTS * SUBLANES, LANES)


def _expert_row(tbl_ref, e):
    r = pl.multiple_of((e & (HALF_EXPERTS - 1)) * SUBLANES, SUBLANES)
    w = tbl_ref[pl.ds(r, SUBLANES), :]
    sh = jnp.where(e >= HALF_EXPERTS, 0, 16).astype(jnp.uint32)
    return pltpu.bitcast((w << sh) & jnp.uint32(0xFFFF0000), jnp.float32)


def _peer_u_body(idx_ref, h_ref, gate_ref, tbl_ref, w_ref):
    sub = lax.broadcasted_iota(jnp.int32, (SUBLANES, LANES), 0)
    lane = lax.broadcasted_iota(jnp.int32, (SUBLANES, LANES), 1)
    low = sub < SUBLANES // 2
    slot = lane - jnp.where(low, 0, N_PICKS // 2)
    npair = N_PICKS // 2

    def token(t, carry):
        hv = h_ref[pl.ds(pl.multiple_of(t * SUBLANES, SUBLANES), SUBLANES), :]
        z = jnp.zeros((SUBLANES, LANES), jnp.float32)
        for j in range(npair):
            pa = _expert_row(tbl_ref, idx_ref[t, j]) * hv
            pb = _expert_row(tbl_ref, idx_ref[t, j + npair]) * hv
            m = jnp.where(low, pa, pb) + pltpu.roll(jnp.where(low, pb, pa), SUBLANES // 2, 0)
            z = jnp.where(slot == j, jnp.sum(m, axis=1, keepdims=True), z)
        a = jnp.sum(z, axis=0, keepdims=True)
        act = 0.5 * a * (1.0 + lax.erf(a * (2.0 ** -0.5)))
        w_ref[pl.ds(t, 1), :] = gate_ref[pl.ds(t, 1), :] * act
        return carry

    lax.fori_loop(0, idx_ref.shape[0], token, 0)


def _peer_u(eidx, h2v, gate, tbl):
    n = eidx.shape[0]
    return pl.pallas_call(
        _peer_u_body,
        grid=(n // TB_PEER,),
        in_specs=[pl.BlockSpec((TB_PEER, N_PICKS), lambda i: (i, 0), memory_space=pltpu.SMEM),
                  pl.BlockSpec((TB_PEER * SUBLANES, LANES), lambda i: (i, 0)),
                  pl.BlockSpec((TB_PEER, N_PICKS), lambda i: (i, 0)),
                  _resident(tbl.shape)],
        out_specs=pl.BlockSpec((TB_PEER, N_PICKS), lambda i: (i, 0)),
        out_shape=jax.ShapeDtypeStruct((n, N_PICKS), jnp.float32),
        compiler_params=_cparams(("arbitrary",)),
        name="peer_u",
    )(eidx, h2v, gate, tbl)


def _peer_v_body(idx_ref, w_ref, x1_ref, tbl_ref, out_ref):
    def token(t, carry):
        acc0 = jnp.zeros((SUBLANES, LANES), jnp.float32)
        acc1 = jnp.zeros((SUBLANES, LANES), jnp.float32)
        for k in range(0, N_PICKS, 2):
            acc0 = acc0 + w_ref[t, k] * _expert_row(tbl_ref, idx_ref[t, k])
            acc1 = acc1 + w_ref[t, k + 1] * _expert_row(tbl_ref, idx_ref[t, k + 1])
        rows = pl.ds(pl.multiple_of(t * SUBLANES, SUBLANES), SUBLANES)
        out_ref[rows, :] = x1_ref[rows, :] + (acc0 + acc1)
        return carry

    lax.fori_loop(0, idx_ref.shape[0], token, 0)


def _peer_v(eidx, w, x1v, tbl):
    n = eidx.shape[0]
    return pl.pallas_call(
        _peer_v_body,
        grid=(n // TB_PEER,),
        in_specs=[pl.BlockSpec((TB_PEER, N_PICKS), lambda i: (i, 0), memory_space=pltpu.SMEM),
                  pl.BlockSpec((TB_PEER, N_PICKS), lambda i: (i, 0), memory_space=pltpu.SMEM),
                  pl.BlockSpec((TB_PEER * SUBLANES, LANES), lambda i: (i, 0)),
                  _resident(tbl.shape)],
        out_specs=pl.BlockSpec((TB_PEER * SUBLANES, LANES), lambda i: (i, 0)),
        out_shape=jax.ShapeDtypeStruct((n * SUBLANES, LANES), jnp.float32),
        compiler_params=_cparams(("arbitrary",)),
        name="peer_v",
    )(eidx, w, x1v, tbl)


def _final_norm_body(x_ref, g_ref, o_ref):
    x = x_ref[...]
    o_ref[...] = x * lax.rsqrt(jnp.mean(x * x, axis=-1, keepdims=True) + EPS) * g_ref[...]


def _final_norm(x2, g):
    n = x2.shape[0]
    return pl.pallas_call(
        _final_norm_body,
        grid=(n // TM_NORM,),
        in_specs=[pl.BlockSpec((TM_NORM, D_MODEL), lambda i: (i, 0)), _resident((1, D_MODEL))],
        out_specs=pl.BlockSpec((TM_NORM, D_MODEL), lambda i: (i, 0)),
        out_shape=jax.ShapeDtypeStruct((n, D_MODEL), jnp.float32),
        compiler_params=_cparams(("parallel",)),
        name="final_norm",
    )(x2, g)


def _reorder_in_cols(a):
    c0 = CONV_CH
    c1 = c0 + CONV_CH
    c2 = c1 + ATTN_W
    c3 = c2 + KV_W
    c4 = c3 + KV_W
    c5 = c4 + D_MODEL
    return jnp.concatenate([a[..., :c2], a[..., c4:c5], a[..., c5:], a[..., c2:c3], a[..., c3:c4]], axis=-1)


def kernel(x, positions, norm1_g, w_in, b_in, conv_w, conv_b, conv_ln_g, conv_ln_b, w_conv_out,
           attn_sinks, w_attn_o, w_out, norm2_g, w_peer_q, peer_sub_keys, peer_u, peer_v, final_g):
    batch, seq, d = x.shape
    n = batch * seq
    depth = w_in.shape[0]
    bf = jnp.bfloat16
    row = lambda a: a.reshape(1, -1)

    x2 = x.reshape(n, d)
    pos_col = positions.reshape(n, 1)
    for l in range(depth):
        z = _in_proj(x2, row(norm1_g[l]), _reorder_in_cols(w_in[l]).astype(bf), row(_reorder_in_cols(b_in[l])))
        cg = _conv_branch(z, batch, seq, conv_w[l], row(conv_b[l]), row(conv_ln_g[l]), row(conv_ln_b[l]),
                          w_conv_out[l].astype(bf))
        ag = _attn_branch(z, pos_col, batch, seq, attn_sinks[l], w_attn_o[l].astype(bf))
        keys = peer_sub_keys[l].reshape(2 * PEER_HEADS, N_KEYS, PEER_DHALF).astype(bf)
        x1, h2, eidx_t, gate_t = _mix_topk(x2, cg, ag, w_out[l].astype(bf), row(norm2_g[l]),
                                           w_peer_q[l].astype(bf), keys)
        eidx = eidx_t.T
        w = _peer_u(eidx, h2.reshape(n * SUBLANES, LANES), gate_t.T, _pack_table(peer_u[l]))
        x2 = _peer_v(eidx, w, x1.reshape(n * SUBLANES, LANES), _pack_table(peer_v[l])).reshape(n, d)
    return _final_norm(x2, row(final_g)).reshape(batch, seq, d)
```

```python
import functools

import jax
import jax.numpy as jnp
import numpy as np
from jax import lax
from jax.experimental import pallas as pl
from jax.experimental.pallas import tpu as pltpu

D_MODEL = 1024
CONV_CH = D_MODEL
CONV_WIDTH = 31
HEAD_DIM = 64
N_Q_HEADS = 16
N_KV_HEADS = 2
GROUP = N_Q_HEADS // N_KV_HEADS
ATTN_W = N_Q_HEADS * HEAD_DIM
KV_W = N_KV_HEADS * HEAD_DIM
WINDOW = 128
BLOCK = 128
ROPE_THETA = 10000.0
PEER_HEADS = 8
N_KEYS = 128
N_EXPERTS = N_KEYS * N_KEYS
PEER_DKEY = 256
PEER_DHALF = PEER_DKEY // 2
PEER_TOPK = 16
N_PICKS = PEER_HEADS * PEER_TOPK
EPS = 1e-6
NEG = -1e30

SUBLANES = 8
LANES = 128
HALF_EXPERTS = N_EXPERTS // 2
VMEM_LIMIT = 56 * 1024 * 1024

Z_COLS = 2 * CONV_CH + ATTN_W + 2 * D_MODEL + 2 * KV_W

TM_PROJ = 256
T_CONV = 512
HALO = 32
TM_MIX = 256
TB_PEER = 128
TM_NORM = 512


def _cparams(sem):
    return pltpu.CompilerParams(dimension_semantics=sem, vmem_limit_bytes=VMEM_LIMIT)


def _resident(shape):
    nd = len(shape)
    return pl.BlockSpec(shape, lambda *_: (0,) * nd, pipeline_mode=pl.Buffered(1))


def _in_proj_body(x_ref, g_ref, w_ref, b_ref, z_ref):
    x = x_ref[...]
    h = x * lax.rsqrt(jnp.mean(x * x, axis=-1, keepdims=True) + EPS) * g_ref[...]
    z_ref[...] = jnp.dot(h.astype(jnp.bfloat16), w_ref[...],
                         preferred_element_type=jnp.float32) + b_ref[...]


def _in_proj(x2, g1, w_in, b_in):
    n = x2.shape[0]
    return pl.pallas_call(
        _in_proj_body,
        grid=(n // TM_PROJ,),
        in_specs=[pl.BlockSpec((TM_PROJ, D_MODEL), lambda i: (i, 0)),
                  _resident((1, D_MODEL)),
                  _resident((D_MODEL, Z_COLS)),
                  _resident((1, Z_COLS))],
        out_specs=pl.BlockSpec((TM_PROJ, Z_COLS), lambda i: (i, 0)),
        out_shape=jax.ShapeDtypeStruct((n, Z_COLS), jnp.float32),
        compiler_params=_cparams(("parallel",)),
        name="in_proj",
    )(x2, g1, w_in, b_in)


def _conv_body(val_ref, gate_ref, gc_ref, cw_ref, cb_ref, lg_ref, lb_ref, wo_ref, out_ref, ext_ref):
    s = pl.program_id(1)
    t = val_ref.shape[0]

    @pl.when(s == 0)
    def _():
        ext_ref[0:HALO, :] = jnp.zeros((HALO, CONV_CH), jnp.float32)

    @pl.when(s != 0)
    def _():
        ext_ref[0:HALO, :] = ext_ref[t:t + HALO, :]

    ext_ref[HALO:HALO + t, :] = val_ref[...] * jax.nn.sigmoid(gate_ref[...])

    base = HALO - (CONV_WIDTH - 1)
    y = jnp.zeros((t, CONV_CH), jnp.float32) + cb_ref[...]
    for j in range(CONV_WIDTH):
        y = y + cw_ref[j:j + 1, :] * ext_ref[base + j:base + j + t, :]

    mu = jnp.mean(y, axis=-1, keepdims=True)
    yc = y - mu
    yn = yc * lax.rsqrt(jnp.mean(yc * yc, axis=-1, keepdims=True) + EPS)
    yn = yn * lg_ref[...] + lb_ref[...]
    act = yn * jax.nn.sigmoid(yn)
    co = jnp.dot(act.astype(jnp.bfloat16), wo_ref[...], preferred_element_type=jnp.float32)
    out_ref[...] = jax.nn.sigmoid(gc_ref[...]) * co


def _conv_branch(z, batch, seq, conv_w, conv_b, ln_g, ln_b, w_conv_out):
    n = z.shape[0]
    nt = seq // T_CONV
    row = lambda b, s: b * nt + s
    return pl.pallas_call(
        _conv_body,
        grid=(batch, nt),
        in_specs=[pl.BlockSpec((T_CONV, CONV_CH), lambda b, s: (row(b, s), 0)),
                  pl.BlockSpec((T_CONV, CONV_CH), lambda b, s: (row(b, s), 1)),
                  pl.BlockSpec((T_CONV, D_MODEL), lambda b, s: (row(b, s), 3)),
                  _resident((CONV_WIDTH, CONV_CH)),
                  _resident((1, CONV_CH)),
                  _resident((1, CONV_CH)),
                  _resident((1, CONV_CH)),
                  _resident((CONV_CH, D_MODEL))],
        out_specs=pl.BlockSpec((T_CONV, D_MODEL), lambda b, s: (row(b, s), 0)),
        out_shape=jax.ShapeDtypeStruct((n, D_MODEL), jnp.float32),
        scratch_shapes=[pltpu.VMEM((T_CONV + HALO, CONV_CH), jnp.float32)],
        compiler_params=_cparams(("arbitrary", "arbitrary")),
        name="conv_branch",
    )(z, z, z, conv_w, conv_b, ln_g, ln_b, w_conv_out)


def _rot_half(x, first_half):
    return jnp.where(first_half, pltpu.roll(x, LANES - HEAD_DIM // 2, 1), pltpu.roll(x, HEAD_DIM // 2, 1))


def _rope_inv_freq():
    half = HEAD_DIM // 2
    f = (np.arange(LANES) % half).astype(np.float32)
    return (np.float32(ROPE_THETA) ** (-(f * np.float32(2.0) / np.float32(HEAD_DIM)))).astype(np.float32)[None, :]


def _attn_body(sinks_ref, pos_ref, inv_ref, q_ref, k_ref, v_ref, ga_ref, wo_ref, out_ref, kp_ref, vp_ref):
    n = pl.program_id(1)
    half = HEAD_DIM // 2

    lane = lax.broadcasted_iota(jnp.int32, (1, LANES), 1)
    ang = pos_ref[...].astype(jnp.float32) * inv_ref[...]
    cos = jnp.cos(ang)
    first_half = (lane % HEAD_DIM) < half
    sin = jnp.where(first_half, -jnp.sin(ang), jnp.sin(ang))

    def rope(x):
        return x * cos + _rot_half(x, first_half) * sin

    k_cur = rope(k_ref[...])
    v_cur = v_ref[...]

    @pl.when(n == 0)
    def _():
        kp_ref[...] = jnp.zeros_like(kp_ref)
        vp_ref[...] = jnp.zeros_like(vp_ref)

    kcat = jnp.concatenate([kp_ref[...], k_cur], axis=0)
    vcat = jnp.concatenate([vp_ref[...], v_cur], axis=0)
    kp_ref[...] = k_cur
    vp_ref[...] = v_cur

    lo = lane < HEAD_DIM
    kvar, vvar = {}, {}
    for g in range(N_KV_HEADS):
        own = lo if g == 0 else jnp.logical_not(lo)
        k_own = jnp.where(own, kcat, 0.0)
        v_own = jnp.where(own, vcat, 0.0)
        k_sw = pltpu.roll(k_own, HEAD_DIM, 1)
        v_sw = pltpu.roll(v_own, HEAD_DIM, 1)
        kvar[g] = ((k_own, k_sw) if g == 0 else (k_sw, k_own))
        vvar[g] = ((v_own, v_sw) if g == 0 else (v_sw, v_own))
        kvar[g] = tuple(a.astype(jnp.bfloat16) for a in kvar[g])
        vvar[g] = tuple(a.astype(jnp.bfloat16) for a in vvar[g])

    qi = lax.broadcasted_iota(jnp.int32, (BLOCK, 2 * BLOCK), 0)
    kj = lax.broadcasted_iota(jnp.int32, (BLOCK, 2 * BLOCK), 1)
    rel = qi + BLOCK - kj
    mask = (rel >= 0) & (rel < WINDOW) & ((kj >= BLOCK) | (n > 0))

    chunks = []
    for c in range(ATTN_W // LANES):
        q_c = rope(q_ref[:, c * LANES:(c + 1) * LANES]).astype(jnp.bfloat16)
        o_c = jnp.zeros((BLOCK, LANES), jnp.float32)
        for pos in range(2):
            h = 2 * c + pos
            g = h // GROUP
            s = lax.dot_general(q_c, kvar[g][pos], (((1,), (1,)), ((), ())),
                                preferred_element_type=jnp.float32) * (HEAD_DIM ** -0.5)
            s = jnp.where(mask, s, NEG)
            sk = sinks_ref[h]
            m = jnp.maximum(jnp.max(s, axis=-1, keepdims=True), sk)
            e = jnp.exp(s - m)
            p = e / (jnp.sum(e, axis=-1, keepdims=True) + jnp.exp(sk - m))
            o_c = o_c + jnp.dot(p.astype(jnp.bfloat16), vvar[g][pos],
                                preferred_element_type=jnp.float32)
        chunks.append(o_c)
    attn = jnp.concatenate(chunks, axis=1).astype(jnp.bfloat16)
    ao = jnp.dot(attn, wo_ref[...], preferred_element_type=jnp.float32)
    out_ref[...] = jax.nn.sigmoid(ga_ref[...]) * ao


def _attn_branch(z, pos_col, batch, seq, sinks, w_attn_o):
    n = z.shape[0]
    nb = seq // BLOCK
    row = lambda b, s: b * nb + s
    kcol = (2 * CONV_CH + ATTN_W + 2 * D_MODEL) // KV_W
    return pl.pallas_call(
        _attn_body,
        grid=(batch, nb),
        in_specs=[pl.BlockSpec(memory_space=pltpu.SMEM),
                  pl.BlockSpec((BLOCK, 1), lambda b, s: (row(b, s), 0)),
                  _resident((1, LANES)),
                  pl.BlockSpec((BLOCK, ATTN_W), lambda b, s: (row(b, s), 2)),
                  pl.BlockSpec((BLOCK, KV_W), lambda b, s: (row(b, s), kcol)),
                  pl.BlockSpec((BLOCK, KV_W), lambda b, s: (row(b, s), kcol + 1)),
                  pl.BlockSpec((BLOCK, D_MODEL), lambda b, s: (row(b, s), 4)),
                  _resident((ATTN_W, D_MODEL))],
        out_specs=pl.BlockSpec((BLOCK, D_MODEL), lambda b, s: (row(b, s), 0)),
        out_shape=jax.ShapeDtypeStruct((n, D_MODEL), jnp.float32),
        scratch_shapes=[pltpu.VMEM((BLOCK, KV_W), jnp.float32),
                        pltpu.VMEM((BLOCK, KV_W), jnp.float32)],
        compiler_params=_cparams(("arbitrary", "arbitrary")),
        name="attn_branch",
    )(sinks, pos_col, jnp.asarray(_rope_inv_freq()), z, z, z, z, w_attn_o)


def _topk_rows(vals, k):
    r = vals.shape[0]
    rows = lax.broadcasted_iota(jnp.int32, vals.shape, 0)
    top_v, top_i = [], []
    for _ in range(k):
        m = jnp.max(vals, axis=0, keepdims=True)
        idx = jnp.min(jnp.where(vals == m, rows, r), axis=0, keepdims=True)
        top_v.append(m)
        top_i.append(idx)
        vals = jnp.where(rows == idx, -jnp.inf, vals)
    return jnp.concatenate(top_v, axis=0), jnp.concatenate(top_i, axis=0)


def _take_rows(table, idx):
    rows = lax.broadcasted_iota(jnp.int32, table.shape, 0)
    return jnp.sum(jnp.where(rows == idx, table, 0), axis=0, keepdims=True)


def _pair_candidates(s1, s2):
    k = PEER_TOPK
    t = s1.shape[1]
    r8 = lax.broadcasted_iota(jnp.int32, (SUBLANES, t), 0)
    r16 = lax.broadcasted_iota(jnp.int32, (k, t), 0)
    vals = [s1[0:1, :] + s2, s1[1:2, :] + s2[0:SUBLANES, :], s1[SUBLANES:k, :] + s2[0:1, :]]
    flat = [r16, k + r8, (r8 + SUBLANES) * k]
    for b in range(k // 3):
        keep = (r8 >= 2) & (r8 < k // (b + 1))
        vals.append(jnp.where(keep, s1[0:SUBLANES, :] + s2[b:b + 1, :], -jnp.inf))
        flat.append(jnp.where(keep, r8 * k + b, k * k))
    return jnp.concatenate(vals, axis=0), jnp.concatenate(flat, axis=0)


def _topk_flat(vals, flat, k):
    big = PEER_TOPK * PEER_TOPK
    top_v, top_p = [], []
    for _ in range(k):
        m = jnp.max(vals, axis=0, keepdims=True)
        p = jnp.min(jnp.where(vals == m, flat, big), axis=0, keepdims=True)
        top_v.append(m)
        top_p.append(p)
        vals = jnp.where(flat == p, -jnp.inf, vals)
    return jnp.concatenate(top_v, axis=0), jnp.concatenate(top_p, axis=0)


def _mix_topk_body(x_ref, cg_ref, ag_ref, wout_ref, g2_ref, wpq_ref, keys_ref,
                   x1_ref, h2_ref, roff_ref, sh_ref, gate_ref):
    merged = (cg_ref[...] + ag_ref[...]).astype(jnp.bfloat16)
    x1 = x_ref[...] + jnp.dot(merged, wout_ref[...], preferred_element_type=jnp.float32)
    x1_ref[...] = x1
    h2 = x1 * lax.rsqrt(jnp.mean(x1 * x1, axis=-1, keepdims=True) + EPS) * g2_ref[...]
    h2_ref[...] = h2
    qp = jnp.dot(h2.astype(jnp.bfloat16), wpq_ref[...],
                 preferred_element_type=jnp.float32).astype(jnp.bfloat16)

    for h in range(PEER_HEADS):
        tops = []
        for c in range(2):
            gi = 2 * h + c
            q_g = qp[:, gi * PEER_DHALF:(gi + 1) * PEER_DHALF]
            sc = lax.dot_general(keys_ref[gi], q_g, (((1,), (1,)), ((), ())),
                                 preferred_element_type=jnp.float32)
            tops.append(_topk_rows(sc, PEER_TOPK))
        (s1, i1), (s2, i2) = tops
        best, pos = _topk_flat(*_pair_candidates(s1, s2), PEER_TOPK)
        e = jnp.exp(best - best[0:1, :])
        rows = slice(h * PEER_TOPK, (h + 1) * PEER_TOPK)
        gate_ref[rows, :] = e / jnp.sum(e, axis=0, keepdims=True)
        picks = []
        for j in range(PEER_TOPK):
            p = pos[j:j + 1, :]
            ia = _take_rows(i1, p >> 4)
            ib = _take_rows(i2, p & (PEER_TOPK - 1))
            picks.append(ia * N_KEYS + ib)
        eidx = jnp.concatenate(picks, axis=0)
        roff_ref[rows, :] = (eidx & (HALF_EXPERTS - 1)) * SUBLANES
        sh_ref[rows, :] = jnp.where(eidx >= HALF_EXPERTS, 16, 0)


def _mix_topk(x2, cg, ag, w_out, g2, w_pq, keys):
    n = x2.shape[0]
    tile = lambda: pl.BlockSpec((TM_MIX, D_MODEL), lambda i: (i, 0))
    colt = lambda: pl.BlockSpec((N_PICKS, TM_MIX), lambda i: (0, i))
    return pl.pallas_call(
        _mix_topk_body,
        grid=(n // TM_MIX,),
        in_specs=[tile(), tile(), tile(),
                  _resident((D_MODEL, D_MODEL)),
                  _resident((1, D_MODEL)),
                  _resident((D_MODEL, PEER_HEADS * PEER_DKEY)),
                  _resident((2 * PEER_HEADS, N_KEYS, PEER_DHALF))],
        out_specs=[tile(), tile(), colt(), colt(), colt()],
        out_shape=[jax.ShapeDtypeStruct((n, D_MODEL), jnp.float32),
                   jax.ShapeDtypeStruct((n, D_MODEL), jnp.float32),
                   jax.ShapeDtypeStruct((N_PICKS, n), jnp.int32),
                   jax.ShapeDtypeStruct((N_PICKS, n), jnp.int32),
                   jax.ShapeDtypeStruct((N_PICKS, n), jnp.float32)],
        compiler_params=_cparams(("parallel",)),
        name="mix_topk",
    )(x2, cg, ag, w_out, g2, w_pq, keys)


def _pack_table(tbl):
    bits = lax.bitcast_convert_type(tbl.astype(jnp.bfloat16), jnp.uint16).astype(jnp.uint32)
    packed = (bits[:HALF_EXPERTS] << 16) | bits[HALF_EXPERTS:]
    return lax.bitcast_convert_type(packed, jnp.int32).reshape(HALF_EXPERTS * SUBLANES, LANES)


def _expert_row(tbl_ref, roff, shift):
    w = tbl_ref[pl.ds(pl.multiple_of(roff, SUBLANES), SUBLANES), :]
    return pltpu.bitcast((w << shift) & jnp.int32(-65536), jnp.float32)


def _lane_splat_rows(vals, n_parts):
    eye = (lax.broadcasted_iota(jnp.int32, (LANES, LANES), 0)
           == lax.broadcasted_iota(jnp.int32, (LANES, LANES), 1))
    ones = jnp.ones((n_parts * LANES, LANES), jnp.bfloat16)
    pieces, rem = [], vals
    for p in range(n_parts):
        piece = rem.astype(jnp.bfloat16).astype(jnp.float32)
        pieces.append(piece)
        rem = rem - piece
    lhs = jnp.concatenate(
        [jnp.concatenate([jnp.where(eye, piece[i:i + 1, :], 0.0) for piece in pieces], axis=1)
         for i in range(SUBLANES)], axis=0)
    return jnp.dot(lhs.astype(jnp.bfloat16), ones, preferred_element_type=jnp.float32)


def _fill_splats(src_ref, dst_ref, n_parts):
    group_rows = SUBLANES * N_PICKS

    def group(g, carry):
        vals = src_ref[pl.ds(pl.multiple_of(g * SUBLANES, SUBLANES), SUBLANES), :].astype(jnp.float32)
        dst_ref[pl.ds(pl.multiple_of(g * group_rows, group_rows), group_rows), :] = (
            _lane_splat_rows(vals, n_parts).astype(dst_ref.dtype))
        return carry

    lax.fori_loop(0, src_ref.shape[0] // SUBLANES, group, 0, unroll=2)


def _peer_u_body(roff_ref, sh_ref, h_ref, gate_ref, tbl_ref, w_ref, srep_ref, a_ref):
    sub = lax.broadcasted_iota(jnp.int32, (SUBLANES, LANES), 0)
    lane = lax.broadcasted_iota(jnp.int32, (SUBLANES, LANES), 1)
    low = sub < SUBLANES // 2
    slot = lane - jnp.where(low, 0, N_PICKS // 2)
    npair = N_PICKS // 2

    _fill_splats(sh_ref, srep_ref, 1)

    def token(t, carry):
        hv = h_ref[pl.ds(pl.multiple_of(t * SUBLANES, SUBLANES), SUBLANES), :]
        base = t * N_PICKS
        z = jnp.zeros((SUBLANES, LANES), jnp.float32)
        for j in range(npair):
            ka, kb = j, j + npair
            pa = _expert_row(tbl_ref, roff_ref[t, ka], srep_ref[pl.ds(base + ka, 1), :]) * hv
            pb = _expert_row(tbl_ref, roff_ref[t, kb], srep_ref[pl.ds(base + kb, 1), :]) * hv
            m = jnp.where(low, pa, pb) + pltpu.roll(jnp.where(low, pb, pa), SUBLANES // 2, 0)
            z = jnp.where(slot == j, jnp.sum(m, axis=1, keepdims=True), z)
        a_ref[pl.ds(t, 1), :] = jnp.sum(z, axis=0, keepdims=True)
        return carry

    lax.fori_loop(0, roff_ref.shape[0], token, 0, unroll=4)
    a = a_ref[...]
    w_ref[...] = gate_ref[...] * (0.5 * a * (1.0 + lax.erf(a * (2.0 ** -0.5))))


def _peer_specs():
    picks_smem = pl.BlockSpec((TB_PEER, N_PICKS), lambda i: (i, 0), memory_space=pltpu.SMEM)
    picks_vmem = pl.BlockSpec((TB_PEER, N_PICKS), lambda i: (i, 0))
    token_tiles = pl.BlockSpec((TB_PEER * SUBLANES, LANES), lambda i: (i, 0))
    return picks_smem, picks_vmem, token_tiles


def _peer_u(roff, sh, h2v, gate, tbl):
    n = roff.shape[0]
    picks_smem, picks_vmem, token_tiles = _peer_specs()
    return pl.pallas_call(
        _peer_u_body,
        grid=(n // TB_PEER,),
        in_specs=[picks_smem, picks_vmem, token_tiles, picks_vmem, _resident(tbl.shape)],
        out_specs=picks_vmem,
        out_shape=jax.ShapeDtypeStruct((n, N_PICKS), jnp.float32),
        scratch_shapes=[pltpu.VMEM((TB_PEER * N_PICKS, LANES), jnp.int32),
                        pltpu.VMEM((TB_PEER, N_PICKS), jnp.float32)],
        compiler_params=_cparams(("arbitrary",)),
        name="peer_u",
    )(roff, sh, h2v, gate, tbl)


def _peer_v_body(roff_ref, sh_ref, w_ref, x1_ref, tbl_ref, out_ref, srep_ref, wrep_ref):
    _fill_splats(sh_ref, srep_ref, 1)
    _fill_splats(w_ref, wrep_ref, 2)

    def token(t, carry):
        base = t * N_PICKS
        acc = [jnp.zeros((SUBLANES, LANES), jnp.float32) for _ in range(2)]
        for k in range(N_PICKS):
            row = _expert_row(tbl_ref, roff_ref[t, k], srep_ref[pl.ds(base + k, 1), :])
            acc[k % 2] = acc[k % 2] + wrep_ref[pl.ds(base + k, 1), :] * row
        rows = pl.ds(pl.multiple_of(t * SUBLANES, SUBLANES), SUBLANES)
        out_ref[rows, :] = x1_ref[rows, :] + (acc[0] + acc[1])
        return carry

    lax.fori_loop(0, roff_ref.shape[0], token, 0)


def _peer_v(roff, sh, w, x1v, tbl):
    n = roff.shape[0]
    picks_smem, picks_vmem, token_tiles = _peer_specs()
    return pl.pallas_call(
        _peer_v_body,
        grid=(n // TB_PEER,),
        in_specs=[picks_smem, picks_vmem, picks_vmem, token_tiles, _resident(tbl.shape)],
        out_specs=token_tiles,
        out_shape=jax.ShapeDtypeStruct((n * SUBLANES, LANES), jnp.float32),
        scratch_shapes=[pltpu.VMEM((TB_PEER * N_PICKS, LANES), jnp.int32),
                        pltpu.VMEM((TB_PEER * N_PICKS, LANES), jnp.float32)],
        compiler_params=_cparams(("arbitrary",)),
        name="peer_v",
    )(roff, sh, w, x1v, tbl)


def _final_norm_body(x_ref, g_ref, o_ref):
    x = x_ref[...]
    o_ref[...] = x * lax.rsqrt(jnp.mean(x * x, axis=-1, keepdims=True) + EPS) * g_ref[...]


def _final_norm(x2, g):
    n = x2.shape[0]
    return pl.pallas_call(
        _final_norm_body,
        grid=(n // TM_NORM,),
        in_specs=[pl.BlockSpec((TM_NORM, D_MODEL), lambda i: (i, 0)), _resident((1, D_MODEL))],
        out_specs=pl.BlockSpec((TM_NORM, D_MODEL), lambda i: (i, 0)),
        out_shape=jax.ShapeDtypeStruct((n, D_MODEL), jnp.float32),
        compiler_params=_cparams(("parallel",)),
        name="final_norm",
    )(x2, g)


def _reorder_in_cols(a):
    c0 = CONV_CH
    c1 = c0 + CONV_CH
    c2 = c1 + ATTN_W
    c3 = c2 + KV_W
    c4 = c3 + KV_W
    c5 = c4 + D_MODEL
    return jnp.concatenate([a[..., :c2], a[..., c4:c5], a[..., c5:], a[..., c2:c3], a[..., c3:c4]], axis=-1)


def kernel(x, positions, norm1_g, w_in, b_in, conv_w, conv_b, conv_ln_g, conv_ln_b, w_conv_out,
           attn_sinks, w_attn_o, w_out, norm2_g, w_peer_q, peer_sub_keys, peer_u, peer_v, final_g):
    batch, seq, d = x.shape
    n = batch * seq
    depth = w_in.shape[0]
    bf = jnp.bfloat16
    row = lambda a: a.reshape(1, -1)

    x2 = x.reshape(n, d)
    pos_col = positions.reshape(n, 1)
    for l in range(depth):
        z = _in_proj(x2, row(norm1_g[l]), _reorder_in_cols(w_in[l]).astype(bf), row(_reorder_in_cols(b_in[l])))
        cg = _conv_branch(z, batch, seq, conv_w[l], row(conv_b[l]), row(conv_ln_g[l]), row(conv_ln_b[l]),
                          w_conv_out[l].astype(bf))
        ag = _attn_branch(z, pos_col, batch, seq, attn_sinks[l], w_attn_o[l].astype(bf))
        keys = peer_sub_keys[l].reshape(2 * PEER_HEADS, N_KEYS, PEER_DHALF).astype(bf)
        x1, h2, roff_t, sh_t, gate_t = _mix_topk(x2, cg, ag, w_out[l].astype(bf), row(norm2_g[l]),
                                                 w_peer_q[l].astype(bf), keys)
        roff, sh = roff_t.T, sh_t.T
        w = _peer_u(roff, sh, h2.reshape(n * SUBLANES, LANES), gate_t.T, _pack_table(peer_u[l]))
        x2 = _peer_v(roff, sh, w, x1.reshape(n * SUBLANES, LANES), _pack_table(peer_v[l])).reshape(n, d)
    return _final_norm(x2, row(final_g)).reshape(batch, seq, d)
```

```python
import functools

import jax
import jax.numpy as jnp
import numpy as np
from jax import lax
from jax.experimental import pallas as pl
from jax.experimental.pallas import tpu as pltpu

D_MODEL = 1024
CONV_CH = D_MODEL
CONV_WIDTH = 31
HEAD_DIM = 64
N_Q_HEADS = 16
N_KV_HEADS = 2
GROUP = N_Q_HEADS // N_KV_HEADS
ATTN_W = N_Q_HEADS * HEAD_DIM
KV_W = N_KV_HEADS * HEAD_DIM
WINDOW = 128
BLOCK = 128
ROPE_THETA = 10000.0
PEER_HEADS = 8
N_KEYS = 128
N_EXPERTS = N_KEYS * N_KEYS
PEER_DKEY = 256
PEER_DHALF = PEER_DKEY // 2
PEER_TOPK = 16
N_PICKS = PEER_HEADS * PEER_TOPK
EPS = 1e-6
NEG = -1e30

SUBLANES = 8
LANES = 128
HALF_EXPERTS = N_EXPERTS // 2
VMEM_LIMIT = 56 * 1024 * 1024

Z_COLS = 2 * CONV_CH + ATTN_W + 2 * D_MODEL + 2 * KV_W

TM_PROJ = 256
T_CONV = 512
HALO = 32
TM_MIX = 256
TB_PEER = 128
CHUNK_PEER_U = 8
CHUNK_PEER_V = 2
TM_NORM = 512


def _cparams(sem):
    return pltpu.CompilerParams(dimension_semantics=sem, vmem_limit_bytes=VMEM_LIMIT)


def _resident(shape):
    nd = len(shape)
    return pl.BlockSpec(shape, lambda *_: (0,) * nd, pipeline_mode=pl.Buffered(1))


def _in_proj_body(x_ref, g_ref, w_ref, b_ref, z_ref):
    x = x_ref[...]
    h = x * lax.rsqrt(jnp.mean(x * x, axis=-1, keepdims=True) + EPS) * g_ref[...]
    z_ref[...] = jnp.dot(h.astype(jnp.bfloat16), w_ref[...],
                         preferred_element_type=jnp.float32) + b_ref[...]


def _in_proj(x2, g1, w_in, b_in):
    n = x2.shape[0]
    return pl.pallas_call(
        _in_proj_body,
        grid=(n // TM_PROJ,),
        in_specs=[pl.BlockSpec((TM_PROJ, D_MODEL), lambda i: (i, 0)),
                  _resident((1, D_MODEL)),
                  _resident((D_MODEL, Z_COLS)),
                  _resident((1, Z_COLS))],
        out_specs=pl.BlockSpec((TM_PROJ, Z_COLS), lambda i: (i, 0)),
        out_shape=jax.ShapeDtypeStruct((n, Z_COLS), jnp.float32),
        compiler_params=_cparams(("parallel",)),
        name="in_proj",
    )(x2, g1, w_in, b_in)


def _conv_body(val_ref, gate_ref, gc_ref, cw_ref, cb_ref, lg_ref, lb_ref, wo_ref, out_ref, ext_ref):
    s = pl.program_id(1)
    t = val_ref.shape[0]

    @pl.when(s == 0)
    def _():
        ext_ref[0:HALO, :] = jnp.zeros((HALO, CONV_CH), jnp.float32)

    @pl.when(s != 0)
    def _():
        ext_ref[0:HALO, :] = ext_ref[t:t + HALO, :]

    ext_ref[HALO:HALO + t, :] = val_ref[...] * jax.nn.sigmoid(gate_ref[...])

    base = HALO - (CONV_WIDTH - 1)
    y = jnp.zeros((t, CONV_CH), jnp.float32) + cb_ref[...]
    for j in range(CONV_WIDTH):
        y = y + cw_ref[j:j + 1, :] * ext_ref[base + j:base + j + t, :]

    mu = jnp.mean(y, axis=-1, keepdims=True)
    yc = y - mu
    yn = yc * lax.rsqrt(jnp.mean(yc * yc, axis=-1, keepdims=True) + EPS)
    yn = yn * lg_ref[...] + lb_ref[...]
    act = yn * jax.nn.sigmoid(yn)
    co = jnp.dot(act.astype(jnp.bfloat16), wo_ref[...], preferred_element_type=jnp.float32)
    out_ref[...] = jax.nn.sigmoid(gc_ref[...]) * co


def _conv_branch(z, batch, seq, conv_w, conv_b, ln_g, ln_b, w_conv_out):
    n = z.shape[0]
    nt = seq // T_CONV
    row = lambda b, s: b * nt + s
    return pl.pallas_call(
        _conv_body,
        grid=(batch, nt),
        in_specs=[pl.BlockSpec((T_CONV, CONV_CH), lambda b, s: (row(b, s), 0)),
                  pl.BlockSpec((T_CONV, CONV_CH), lambda b, s: (row(b, s), 1)),
                  pl.BlockSpec((T_CONV, D_MODEL), lambda b, s: (row(b, s), 3)),
                  _resident((CONV_WIDTH, CONV_CH)),
                  _resident((1, CONV_CH)),
                  _resident((1, CONV_CH)),
                  _resident((1, CONV_CH)),
                  _resident((CONV_CH, D_MODEL))],
        out_specs=pl.BlockSpec((T_CONV, D_MODEL), lambda b, s: (row(b, s), 0)),
        out_shape=jax.ShapeDtypeStruct((n, D_MODEL), jnp.float32),
        scratch_shapes=[pltpu.VMEM((T_CONV + HALO, CONV_CH), jnp.float32)],
        compiler_params=_cparams(("arbitrary", "arbitrary")),
        name="conv_branch",
    )(z, z, z, conv_w, conv_b, ln_g, ln_b, w_conv_out)


def _rot_half(x, first_half):
    return jnp.where(first_half, pltpu.roll(x, LANES - HEAD_DIM // 2, 1), pltpu.roll(x, HEAD_DIM // 2, 1))


def _rope_inv_freq():
    half = HEAD_DIM // 2
    f = (np.arange(LANES) % half).astype(np.float32)
    return (np.float32(ROPE_THETA) ** (-(f * np.float32(2.0) / np.float32(HEAD_DIM)))).astype(np.float32)[None, :]


def _attn_body(sinks_ref, pos_ref, inv_ref, q_ref, k_ref, v_ref, ga_ref, wo_ref, out_ref, kp_ref, vp_ref):
    n = pl.program_id(1)
    half = HEAD_DIM // 2

    lane = lax.broadcasted_iota(jnp.int32, (1, LANES), 1)
    ang = pos_ref[...].astype(jnp.float32) * inv_ref[...]
    cos = jnp.cos(ang)
    first_half = (lane % HEAD_DIM) < half
    sin = jnp.where(first_half, -jnp.sin(ang), jnp.sin(ang))

    def rope(x):
        return x * cos + _rot_half(x, first_half) * sin

    k_cur = rope(k_ref[...])
    v_cur = v_ref[...]

    @pl.when(n == 0)
    def _():
        kp_ref[...] = jnp.zeros_like(kp_ref)
        vp_ref[...] = jnp.zeros_like(vp_ref)

    kcat = jnp.concatenate([kp_ref[...], k_cur], axis=0)
    vcat = jnp.concatenate([vp_ref[...], v_cur], axis=0)
    kp_ref[...] = k_cur
    vp_ref[...] = v_cur

    lo = lane < HEAD_DIM
    kvar, vvar = {}, {}
    for g in range(N_KV_HEADS):
        own = lo if g == 0 else jnp.logical_not(lo)
        k_own = jnp.where(own, kcat, 0.0)
        v_own = jnp.where(own, vcat, 0.0)
        k_sw = pltpu.roll(k_own, HEAD_DIM, 1)
        v_sw = pltpu.roll(v_own, HEAD_DIM, 1)
        kvar[g] = ((k_own, k_sw) if g == 0 else (k_sw, k_own))
        vvar[g] = ((v_own, v_sw) if g == 0 else (v_sw, v_own))
        kvar[g] = tuple(a.astype(jnp.bfloat16) for a in kvar[g])
        vvar[g] = tuple(a.astype(jnp.bfloat16) for a in vvar[g])

    qi = lax.broadcasted_iota(jnp.int32, (BLOCK, 2 * BLOCK), 0)
    kj = lax.broadcasted_iota(jnp.int32, (BLOCK, 2 * BLOCK), 1)
    rel = qi + BLOCK - kj
    mask = (rel >= 0) & (rel < WINDOW) & ((kj >= BLOCK) | (n > 0))

    chunks = []
    for c in range(ATTN_W // LANES):
        q_c = rope(q_ref[:, c * LANES:(c + 1) * LANES]).astype(jnp.bfloat16)
        o_c = jnp.zeros((BLOCK, LANES), jnp.float32)
        for pos in range(2):
            h = 2 * c + pos
            g = h // GROUP
            s = lax.dot_general(q_c, kvar[g][pos], (((1,), (1,)), ((), ())),
                                preferred_element_type=jnp.float32) * (HEAD_DIM ** -0.5)
            s = jnp.where(mask, s, NEG)
            sk = sinks_ref[h]
            m = jnp.maximum(jnp.max(s, axis=-1, keepdims=True), sk)
            e = jnp.exp(s - m)
            p = e / (jnp.sum(e, axis=-1, keepdims=True) + jnp.exp(sk - m))
            o_c = o_c + jnp.dot(p.astype(jnp.bfloat16), vvar[g][pos],
                                preferred_element_type=jnp.float32)
        chunks.append(o_c)
    attn = jnp.concatenate(chunks, axis=1).astype(jnp.bfloat16)
    ao = jnp.dot(attn, wo_ref[...], preferred_element_type=jnp.float32)
    out_ref[...] = jax.nn.sigmoid(ga_ref[...]) * ao


def _attn_branch(z, pos_col, batch, seq, sinks, w_attn_o):
    n = z.shape[0]
    nb = seq // BLOCK
    row = lambda b, s: b * nb + s
    kcol = (2 * CONV_CH + ATTN_W + 2 * D_MODEL) // KV_W
    return pl.pallas_call(
        _attn_body,
        grid=(batch, nb),
        in_specs=[pl.BlockSpec(memory_space=pltpu.SMEM),
                  pl.BlockSpec((BLOCK, 1), lambda b, s: (row(b, s), 0)),
                  _resident((1, LANES)),
                  pl.BlockSpec((BLOCK, ATTN_W), lambda b, s: (row(b, s), 2)),
                  pl.BlockSpec((BLOCK, KV_W), lambda b, s: (row(b, s), kcol)),
                  pl.BlockSpec((BLOCK, KV_W), lambda b, s: (row(b, s), kcol + 1)),
                  pl.BlockSpec((BLOCK, D_MODEL), lambda b, s: (row(b, s), 4)),
                  _resident((ATTN_W, D_MODEL))],
        out_specs=pl.BlockSpec((BLOCK, D_MODEL), lambda b, s: (row(b, s), 0)),
        out_shape=jax.ShapeDtypeStruct((n, D_MODEL), jnp.float32),
        scratch_shapes=[pltpu.VMEM((BLOCK, KV_W), jnp.float32),
                        pltpu.VMEM((BLOCK, KV_W), jnp.float32)],
        compiler_params=_cparams(("arbitrary", "arbitrary")),
        name="attn_branch",
    )(sinks, pos_col, jnp.asarray(_rope_inv_freq()), z, z, z, z, w_attn_o)


def _topk_rows(vals, k):
    neg_rows = -lax.broadcasted_iota(jnp.int32, vals.shape, 0).astype(jnp.float32)
    top_v, top_i = [], []
    for _ in range(k):
        m = jnp.max(vals, axis=0, keepdims=True)
        neg_idx = jnp.max(jnp.where(vals == m, neg_rows, -jnp.inf), axis=0, keepdims=True)
        top_v.append(m)
        top_i.append(neg_idx)
        vals = jnp.where(neg_rows == neg_idx, -jnp.inf, vals)
    return jnp.concatenate(top_v, axis=0), (-jnp.concatenate(top_i, axis=0)).astype(jnp.int32)


def _take_rows(table, idx):
    rows = lax.broadcasted_iota(jnp.int32, table.shape, 0)
    return jnp.sum(jnp.where(rows == idx, table, 0), axis=0, keepdims=True)


def _pair_candidates(s1, s2):
    k = PEER_TOPK
    t = s1.shape[1]
    r8 = lax.broadcasted_iota(jnp.int32, (SUBLANES, t), 0)
    r16 = lax.broadcasted_iota(jnp.int32, (k, t), 0)
    vals = [s1[0:1, :] + s2, s1[1:2, :] + s2[0:SUBLANES, :], s1[SUBLANES:k, :] + s2[0:1, :]]
    flat = [r16, k + r8, (r8 + SUBLANES) * k]
    for b in range(k // 3):
        keep = (r8 >= 2) & (r8 < k // (b + 1))
        vals.append(jnp.where(keep, s1[0:SUBLANES, :] + s2[b:b + 1, :], -jnp.inf))
        flat.append(jnp.where(keep, r8 * k + b, k * k))
    return jnp.concatenate(vals, axis=0), jnp.concatenate(flat, axis=0)


def _topk_flat(vals, flat, k):
    neg_flat = -flat.astype(jnp.float32)
    top_v, top_p = [], []
    for _ in range(k):
        m = jnp.max(vals, axis=0, keepdims=True)
        neg_p = jnp.max(jnp.where(vals == m, neg_flat, -jnp.inf), axis=0, keepdims=True)
        top_v.append(m)
        top_p.append(neg_p)
        vals = jnp.where(neg_flat == neg_p, -jnp.inf, vals)
    return jnp.concatenate(top_v, axis=0), (-jnp.concatenate(top_p, axis=0)).astype(jnp.int32)


def _mix_topk_body(x_ref, cg_ref, ag_ref, wout_ref, g2_ref, wpq_ref, keys_ref,
                   x1_ref, h2_ref, roff_ref, sh_ref, gate_ref):
    merged = (cg_ref[...] + ag_ref[...]).astype(jnp.bfloat16)
    x1 = x_ref[...] + jnp.dot(merged, wout_ref[...], preferred_element_type=jnp.float32)
    x1_ref[...] = x1
    h2 = x1 * lax.rsqrt(jnp.mean(x1 * x1, axis=-1, keepdims=True) + EPS) * g2_ref[...]
    h2_ref[...] = h2
    qp = jnp.dot(h2.astype(jnp.bfloat16), wpq_ref[...],
                 preferred_element_type=jnp.float32).astype(jnp.bfloat16)

    gates, eidxs = [], []
    for h in range(PEER_HEADS):
        tops = []
        for c in range(2):
            gi = 2 * h + c
            q_g = qp[:, gi * PEER_DHALF:(gi + 1) * PEER_DHALF]
            sc = lax.dot_general(keys_ref[gi], q_g, (((1,), (1,)), ((), ())),
                                 preferred_element_type=jnp.float32)
            tops.append(_topk_rows(sc, PEER_TOPK))
        (s1, i1), (s2, i2) = tops
        best, pos = _topk_flat(*_pair_candidates(s1, s2), PEER_TOPK)
        e = jnp.exp(best - best[0:1, :])
        gates.append(e / jnp.sum(e, axis=0, keepdims=True))
        for j in range(PEER_TOPK):
            p = pos[j:j + 1, :]
            ia = _take_rows(i1, p >> 4)
            ib = _take_rows(i2, p & (PEER_TOPK - 1))
            eidxs.append(ia * N_KEYS + ib)
    gate_ref[...] = jnp.concatenate(gates, axis=0).T
    eidx = jnp.concatenate(eidxs, axis=0).astype(jnp.float32).T.astype(jnp.int32)
    roff_ref[...] = (eidx & (HALF_EXPERTS - 1)) * SUBLANES
    sh_ref[...] = jnp.where(eidx >= HALF_EXPERTS, 16, 0)


def _mix_topk(x2, cg, ag, w_out, g2, w_pq, keys):
    n = x2.shape[0]
    tile = lambda: pl.BlockSpec((TM_MIX, D_MODEL), lambda i: (i, 0))
    colt = lambda: pl.BlockSpec((TM_MIX, N_PICKS), lambda i: (i, 0))
    return pl.pallas_call(
        _mix_topk_body,
        grid=(n // TM_MIX,),
        in_specs=[tile(), tile(), tile(),
                  _resident((D_MODEL, D_MODEL)),
                  _resident((1, D_MODEL)),
                  _resident((D_MODEL, PEER_HEADS * PEER_DKEY)),
                  _resident((2 * PEER_HEADS, N_KEYS, PEER_DHALF))],
        out_specs=[tile(), tile(), colt(), colt(), colt()],
        out_shape=[jax.ShapeDtypeStruct((n, D_MODEL), jnp.float32),
                   jax.ShapeDtypeStruct((n, D_MODEL), jnp.float32),
                   jax.ShapeDtypeStruct((n, N_PICKS), jnp.int32),
                   jax.ShapeDtypeStruct((n, N_PICKS), jnp.int32),
                   jax.ShapeDtypeStruct((n, N_PICKS), jnp.float32)],
        compiler_params=_cparams(("parallel",)),
        name="mix_topk",
    )(x2, cg, ag, w_out, g2, w_pq, keys)


def _pack_table(tbl):
    bits = lax.bitcast_convert_type(tbl.astype(jnp.bfloat16), jnp.uint16).astype(jnp.uint32)
    packed = (bits[:HALF_EXPERTS] << 16) | bits[HALF_EXPERTS:]
    return lax.bitcast_convert_type(packed, jnp.int32).reshape(HALF_EXPERTS * SUBLANES, LANES)


def _expert_row(tbl_ref, roff, shift):
    w = tbl_ref[pl.ds(pl.multiple_of(roff, SUBLANES), SUBLANES), :]
    return pltpu.bitcast((w << shift) & jnp.int32(-65536), jnp.float32)


def _lane_splat_rows(rows, n_parts):
    eye = (lax.broadcasted_iota(jnp.int32, (LANES, LANES), 0)
           == lax.broadcasted_iota(jnp.int32, (LANES, LANES), 1))
    ones = jnp.ones((n_parts * LANES, LANES), jnp.bfloat16)
    blocks = []
    for rem in rows:
        pieces = []
        for _ in range(n_parts):
            piece = rem.astype(jnp.bfloat16).astype(jnp.float32)
            pieces.append(jnp.where(eye, piece, 0.0))
            rem = rem - piece
        blocks.append(jnp.concatenate(pieces, axis=1))
    lhs = jnp.concatenate(blocks, axis=0)
    return jnp.dot(lhs.astype(jnp.bfloat16), ones, preferred_element_type=jnp.float32)


def _splat_chunk(src_ref, c, dst_ref, n_parts):
    chunk = dst_ref.shape[0] // N_PICKS
    rows = [src_ref[pl.ds(c * chunk + i, 1), :].astype(jnp.float32) for i in range(chunk)]
    dst_ref[...] = _lane_splat_rows(rows, n_parts).astype(dst_ref.dtype)


def _chunked_token_loop(n_chunks, fill, tokens):
    fill(0, 0)

    def pair(i, carry):
        c = 2 * i
        fill(c + 1, 1)
        tokens(c, 0)
        fill(jnp.minimum(c + 2, n_chunks - 1), 0)
        tokens(c + 1, 1)
        return carry

    lax.fori_loop(0, n_chunks // 2, pair, 0)


def _peer_u_body(roff_ref, sh_ref, h_ref, gate_ref, tbl_ref, w_ref, s0_ref, s1_ref, a_ref):
    sub = lax.broadcasted_iota(jnp.int32, (SUBLANES, LANES), 0)
    lane = lax.broadcasted_iota(jnp.int32, (SUBLANES, LANES), 1)
    low = sub < SUBLANES // 2
    slot = lane - jnp.where(low, 0, N_PICKS // 2)
    npair = N_PICKS // 2
    s_refs = (s0_ref, s1_ref)
    chunk = CHUNK_PEER_U

    def fill(c, b):
        _splat_chunk(sh_ref, c, s_refs[b], 1)

    def tokens(c, b):
        s_ref = s_refs[b]
        for i in range(chunk):
            t = c * chunk + i
            hv = h_ref[pl.ds(pl.multiple_of(t * SUBLANES, SUBLANES), SUBLANES), :]
            z = jnp.zeros((SUBLANES, LANES), jnp.float32)
            for j in range(npair):
                ka, kb = i * N_PICKS + j, i * N_PICKS + j + npair
                pa = _expert_row(tbl_ref, roff_ref[t, j], s_ref[ka:ka + 1, :]) * hv
                pb = _expert_row(tbl_ref, roff_ref[t, j + npair], s_ref[kb:kb + 1, :]) * hv
                m = jnp.where(low, pa, pb) + pltpu.roll(jnp.where(low, pb, pa), SUBLANES // 2, 0)
                z = jnp.where(slot == j, jnp.sum(m, axis=1, keepdims=True), z)
            a_ref[pl.ds(t, 1), :] = jnp.sum(z, axis=0, keepdims=True)

    _chunked_token_loop(roff_ref.shape[0] // chunk, fill, tokens)
    a = a_ref[...]
    w_ref[...] = gate_ref[...] * (0.5 * a * (1.0 + lax.erf(a * (2.0 ** -0.5))))


def _peer_specs():
    picks_smem = pl.BlockSpec((TB_PEER, N_PICKS), lambda i: (i, 0), memory_space=pltpu.SMEM)
    picks_vmem = pl.BlockSpec((TB_PEER, N_PICKS), lambda i: (i, 0))
    token_tiles = pl.BlockSpec((TB_PEER * SUBLANES, LANES), lambda i: (i, 0))
    return picks_smem, picks_vmem, token_tiles


def _splat_scratch(chunk, dtype):
    return pltpu.VMEM((chunk * N_PICKS, LANES), dtype)


def _peer_u(roff, sh, h2v, gate, tbl):
    n = roff.shape[0]
    picks_smem, picks_vmem, token_tiles = _peer_specs()
    return pl.pallas_call(
        _peer_u_body,
        grid=(n // TB_PEER,),
        in_specs=[picks_smem, picks_vmem, token_tiles, picks_vmem, _resident(tbl.shape)],
        out_specs=picks_vmem,
        out_shape=jax.ShapeDtypeStruct((n, N_PICKS), jnp.float32),
        scratch_shapes=[_splat_scratch(CHUNK_PEER_U, jnp.int32), _splat_scratch(CHUNK_PEER_U, jnp.int32),
                        pltpu.VMEM((TB_PEER, N_PICKS), jnp.float32)],
        compiler_params=_cparams(("arbitrary",)),
        name="peer_u",
    )(roff, sh, h2v, gate, tbl)


def _peer_v_body(roff_ref, sh_ref, w_ref, x1_ref, tbl_ref, out_ref, s0_ref, s1_ref, w0_ref, w1_ref):
    s_refs = (s0_ref, s1_ref)
    w_refs = (w0_ref, w1_ref)
    chunk = CHUNK_PEER_V

    def fill(c, b):
        _splat_chunk(sh_ref, c, s_refs[b], 1)
        _splat_chunk(w_ref, c, w_refs[b], 2)

    def tokens(c, b):
        s_ref, ws_ref = s_refs[b], w_refs[b]
        for i in range(chunk):
            t = c * chunk + i
            acc = [jnp.zeros((SUBLANES, LANES), jnp.float32) for _ in range(2)]
            for k in range(N_PICKS):
                r = i * N_PICKS + k
                row = _expert_row(tbl_ref, roff_ref[t, k], s_ref[r:r + 1, :])
                acc[k % 2] = acc[k % 2] + ws_ref[r:r + 1, :] * row
            rows = pl.ds(pl.multiple_of(t * SUBLANES, SUBLANES), SUBLANES)
            out_ref[rows, :] = x1_ref[rows, :] + (acc[0] + acc[1])

    _chunked_token_loop(roff_ref.shape[0] // chunk, fill, tokens)


def _peer_v(roff, sh, w, x1v, tbl):
    n = roff.shape[0]
    picks_smem, picks_vmem, token_tiles = _peer_specs()
    return pl.pallas_call(
        _peer_v_body,
        grid=(n // TB_PEER,),
        in_specs=[picks_smem, picks_vmem, picks_vmem, token_tiles, _resident(tbl.shape)],
        out_specs=token_tiles,
        out_shape=jax.ShapeDtypeStruct((n * SUBLANES, LANES), jnp.float32),
        scratch_shapes=[_splat_scratch(CHUNK_PEER_V, jnp.int32), _splat_scratch(CHUNK_PEER_V, jnp.int32),
                        _splat_scratch(CHUNK_PEER_V, jnp.float32), _splat_scratch(CHUNK_PEER_V, jnp.float32)],
        compiler_params=_cparams(("arbitrary",)),
        name="peer_v",
    )(roff, sh, w, x1v, tbl)


def _final_norm_body(x_ref, g_ref, o_ref):
    x = x_ref[...]
    o_ref[...] = x * lax.rsqrt(jnp.mean(x * x, axis=-1, keepdims=True) + EPS) * g_ref[...]


def _final_norm(x2, g):
    n = x2.shape[0]
    return pl.pallas_call(
        _final_norm_body,
        grid=(n // TM_NORM,),
        in_specs=[pl.BlockSpec((TM_NORM, D_MODEL), lambda i: (i, 0)), _resident((1, D_MODEL))],
        out_specs=pl.BlockSpec((TM_NORM, D_MODEL), lambda i: (i, 0)),
        out_shape=jax.ShapeDtypeStruct((n, D_MODEL), jnp.float32),
        compiler_params=_cparams(("parallel",)),
        name="final_norm",
    )(x2, g)


def _reorder_in_cols(a):
    c0 = CONV_CH
    c1 = c0 + CONV_CH
    c2 = c1 + ATTN_W
    c3 = c2 + KV_W
    c4 = c3 + KV_W
    c5 = c4 + D_MODEL
    return jnp.concatenate([a[..., :c2], a[..., c4:c5], a[..., c5:], a[..., c2:c3], a[..., c3:c4]], axis=-1)


def kernel(x, positions, norm1_g, w_in, b_in, conv_w, conv_b, conv_ln_g, conv_ln_b, w_conv_out,
           attn_sinks, w_attn_o, w_out, norm2_g, w_peer_q, peer_sub_keys, peer_u, peer_v, final_g):
    batch, seq, d = x.shape
    n = batch * seq
    depth = w_in.shape[0]
    bf = jnp.bfloat16
    row = lambda a: a.reshape(1, -1)

    x2 = x.reshape(n, d)
    pos_col = positions.reshape(n, 1)
    for l in range(depth):
        z = _in_proj(x2, row(norm1_g[l]), _reorder_in_cols(w_in[l]).astype(bf), row(_reorder_in_cols(b_in[l])))
        cg = _conv_branch(z, batch, seq, conv_w[l], row(conv_b[l]), row(conv_ln_g[l]), row(conv_ln_b[l]),
                          w_conv_out[l].astype(bf))
        ag = _attn_branch(z, pos_col, batch, seq, attn_sinks[l], w_attn_o[l].astype(bf))
        keys = peer_sub_keys[l].reshape(2 * PEER_HEADS, N_KEYS, PEER_DHALF).astype(bf)
        x1, h2, roff, sh, gate = _mix_topk(x2, cg, ag, w_out[l].astype(bf), row(norm2_g[l]),
                                           w_peer_q[l].astype(bf), keys)
        w = _peer_u(roff, sh, h2.reshape(n * SUBLANES, LANES), gate, _pack_table(peer_u[l]))
        x2 = _peer_v(roff, sh, w, x1.reshape(n * SUBLANES, LANES), _pack_table(peer_v[l])).reshape(n, d)
    return _final_norm(x2, row(final_g)).reshape(batch, seq, d)
```

```python
import functools

import jax
import jax.numpy as jnp
import numpy as np
from jax import lax
from jax.experimental import pallas as pl
from jax.experimental.pallas import tpu as pltpu

D_MODEL = 1024
CONV_CH = D_MODEL
CONV_WIDTH = 31
HEAD_DIM = 64
N_Q_HEADS = 16
N_KV_HEADS = 2
GROUP = N_Q_HEADS // N_KV_HEADS
ATTN_W = N_Q_HEADS * HEAD_DIM
KV_W = N_KV_HEADS * HEAD_DIM
WINDOW = 128
BLOCK = 128
ROPE_THETA = 10000.0
PEER_HEADS = 8
N_KEYS = 128
N_EXPERTS = N_KEYS * N_KEYS
PEER_DKEY = 256
PEER_DHALF = PEER_DKEY // 2
PEER_TOPK = 16
N_PICKS = PEER_HEADS * PEER_TOPK
EPS = 1e-6
NEG = -1e30

SUBLANES = 8
LANES = 128
HALF_EXPERTS = N_EXPERTS // 2
VMEM_LIMIT = 56 * 1024 * 1024

Z_COLS = 2 * CONV_CH + ATTN_W + 2 * D_MODEL + 2 * KV_W

TM_PROJ = 256
T_CONV = 512
HALO = 32
CONV_ROWS = 64
TM_MIX = 256
TB_PEER = 128
CHUNK_PEER_U = 8
CHUNK_PEER_V = 2
TM_NORM = 512


def _cparams(sem):
    return pltpu.CompilerParams(dimension_semantics=sem, vmem_limit_bytes=VMEM_LIMIT)


def _resident(shape):
    nd = len(shape)
    return pl.BlockSpec(shape, lambda *_: (0,) * nd, pipeline_mode=pl.Buffered(1))


def _in_proj_body(x_ref, g_ref, w_ref, b_ref, z_ref):
    x = x_ref[...]
    h = x * lax.rsqrt(jnp.mean(x * x, axis=-1, keepdims=True) + EPS) * g_ref[...]
    z_ref[...] = jnp.dot(h.astype(jnp.bfloat16), w_ref[...],
                         preferred_element_type=jnp.float32) + b_ref[...]


def _in_proj(x2, g1, w_in, b_in):
    n = x2.shape[0]
    return pl.pallas_call(
        _in_proj_body,
        grid=(n // TM_PROJ,),
        in_specs=[pl.BlockSpec((TM_PROJ, D_MODEL), lambda i: (i, 0)),
                  _resident((1, D_MODEL)),
                  _resident((D_MODEL, Z_COLS)),
                  _resident((1, Z_COLS))],
        out_specs=pl.BlockSpec((TM_PROJ, Z_COLS), lambda i: (i, 0)),
        out_shape=jax.ShapeDtypeStruct((n, Z_COLS), jnp.float32),
        compiler_params=_cparams(("parallel",)),
        name="in_proj",
    )(x2, g1, w_in, b_in)


def _conv_body(val_ref, gate_ref, gc_ref, cw_ref, cb_ref, lg_ref, lb_ref, wo_ref, out_ref, ext_ref, y_ref):
    s = pl.program_id(1)
    t = val_ref.shape[0]
    n_chunks = CONV_CH // LANES

    @pl.when(s == 0)
    def _():
        ext_ref[:, 0:HALO, :] = jnp.zeros((n_chunks, HALO, LANES), jnp.float32)

    @pl.when(s != 0)
    def _():
        ext_ref[:, 0:HALO, :] = ext_ref[:, t:t + HALO, :]

    for c in range(n_chunks):
        lanes = slice(c * LANES, (c + 1) * LANES)
        ext_ref[c, HALO:HALO + t, :] = val_ref[:, lanes] * jax.nn.sigmoid(gate_ref[:, lanes])

    base = HALO - (CONV_WIDTH - 1)
    for c in range(n_chunks):
        lanes = slice(c * LANES, (c + 1) * LANES)
        for r0 in range(0, t, CONV_ROWS):
            acc = jnp.zeros((CONV_ROWS, LANES), jnp.float32) + cb_ref[:, lanes]
            for j in range(CONV_WIDTH):
                acc = acc + cw_ref[j:j + 1, lanes] * ext_ref[c, base + j + r0:base + j + r0 + CONV_ROWS, :]
            y_ref[r0:r0 + CONV_ROWS, lanes] = acc

    y = y_ref[...]
    mu = jnp.mean(y, axis=-1, keepdims=True)
    yc = y - mu
    yn = yc * lax.rsqrt(jnp.mean(yc * yc, axis=-1, keepdims=True) + EPS)
    yn = yn * lg_ref[...] + lb_ref[...]
    act = yn * jax.nn.sigmoid(yn)
    co = jnp.dot(act.astype(jnp.bfloat16), wo_ref[...], preferred_element_type=jnp.float32)
    out_ref[...] = jax.nn.sigmoid(gc_ref[...]) * co


def _conv_branch(z, batch, seq, conv_w, conv_b, ln_g, ln_b, w_conv_out):
    n = z.shape[0]
    nt = seq // T_CONV
    row = lambda b, s: b * nt + s
    return pl.pallas_call(
        _conv_body,
        grid=(batch, nt),
        in_specs=[pl.BlockSpec((T_CONV, CONV_CH), lambda b, s: (row(b, s), 0)),
                  pl.BlockSpec((T_CONV, CONV_CH), lambda b, s: (row(b, s), 1)),
                  pl.BlockSpec((T_CONV, D_MODEL), lambda b, s: (row(b, s), 3)),
                  _resident((CONV_WIDTH, CONV_CH)),
                  _resident((1, CONV_CH)),
                  _resident((1, CONV_CH)),
                  _resident((1, CONV_CH)),
                  _resident((CONV_CH, D_MODEL))],
        out_specs=pl.BlockSpec((T_CONV, D_MODEL), lambda b, s: (row(b, s), 0)),
        out_shape=jax.ShapeDtypeStruct((n, D_MODEL), jnp.float32),
        scratch_shapes=[pltpu.VMEM((CONV_CH // LANES, T_CONV + HALO, LANES), jnp.float32),
                        pltpu.VMEM((T_CONV, CONV_CH), jnp.float32)],
        compiler_params=_cparams(("arbitrary", "arbitrary")),
        name="conv_branch",
    )(z, z, z, conv_w, conv_b, ln_g, ln_b, w_conv_out)


def _rot_half(x, first_half):
    return jnp.where(first_half, pltpu.roll(x, LANES - HEAD_DIM // 2, 1), pltpu.roll(x, HEAD_DIM // 2, 1))


def _rope_inv_freq():
    half = HEAD_DIM // 2
    f = (np.arange(LANES) % half).astype(np.float32)
    return (np.float32(ROPE_THETA) ** (-(f * np.float32(2.0) / np.float32(HEAD_DIM)))).astype(np.float32)[None, :]


def _attn_body(sinks_ref, pos_ref, inv_ref, q_ref, k_ref, v_ref, ga_ref, wo_ref, out_ref, kp_ref, vp_ref):
    n = pl.program_id(1)
    half = HEAD_DIM // 2

    lane = lax.broadcasted_iota(jnp.int32, (1, LANES), 1)
    ang = pos_ref[...].astype(jnp.float32) * inv_ref[...]
    cos = jnp.cos(ang)
    first_half = (lane % HEAD_DIM) < half
    sin = jnp.where(first_half, -jnp.sin(ang), jnp.sin(ang))

    def rope(x):
        return x * cos + _rot_half(x, first_half) * sin

    k_cur = rope(k_ref[...])
    v_cur = v_ref[...]

    @pl.when(n == 0)
    def _():
        kp_ref[...] = jnp.zeros_like(kp_ref)
        vp_ref[...] = jnp.zeros_like(vp_ref)

    kcat = jnp.concatenate([kp_ref[...], k_cur], axis=0)
    vcat = jnp.concatenate([vp_ref[...], v_cur], axis=0)
    kp_ref[...] = k_cur
    vp_ref[...] = v_cur

    lo = lane < HEAD_DIM
    kvar, vvar = {}, {}
    for g in range(N_KV_HEADS):
        own = lo if g == 0 else jnp.logical_not(lo)
        k_own = jnp.where(own, kcat, 0.0)
        v_own = jnp.where(own, vcat, 0.0)
        k_sw = pltpu.roll(k_own, HEAD_DIM, 1)
        v_sw = pltpu.roll(v_own, HEAD_DIM, 1)
        kvar[g] = ((k_own, k_sw) if g == 0 else (k_sw, k_own))
        vvar[g] = ((v_own, v_sw) if g == 0 else (v_sw, v_own))
        kvar[g] = tuple(a.astype(jnp.bfloat16) for a in kvar[g])
        vvar[g] = tuple(a.astype(jnp.bfloat16) for a in vvar[g])

    qi = lax.broadcasted_iota(jnp.int32, (BLOCK, 2 * BLOCK), 0)
    kj = lax.broadcasted_iota(jnp.int32, (BLOCK, 2 * BLOCK), 1)
    rel = qi + BLOCK - kj
    mask = (rel >= 0) & (rel < WINDOW) & ((kj >= BLOCK) | (n > 0))

    chunks = []
    for c in range(ATTN_W // LANES):
        q_c = rope(q_ref[:, c * LANES:(c + 1) * LANES]).astype(jnp.bfloat16)
        o_c = jnp.zeros((BLOCK, LANES), jnp.float32)
        for pos in range(2):
            h = 2 * c + pos
            g = h // GROUP
            s = lax.dot_general(q_c, kvar[g][pos], (((1,), (1,)), ((), ())),
                                preferred_element_type=jnp.float32) * (HEAD_DIM ** -0.5)
            s = jnp.where(mask, s, NEG)
            sk = sinks_ref[h]
            m = jnp.maximum(jnp.max(s, axis=-1, keepdims=True), sk)
            e = jnp.exp(s - m)
            p = e / (jnp.sum(e, axis=-1, keepdims=True) + jnp.exp(sk - m))
            o_c = o_c + jnp.dot(p.astype(jnp.bfloat16), vvar[g][pos],
                                preferred_element_type=jnp.float32)
        chunks.append(o_c)
    attn = jnp.concatenate(chunks, axis=1).astype(jnp.bfloat16)
    ao = jnp.dot(attn, wo_ref[...], preferred_element_type=jnp.float32)
    out_ref[...] = jax.nn.sigmoid(ga_ref[...]) * ao


def _attn_branch(z, pos_col, batch, seq, sinks, w_attn_o):
    n = z.shape[0]
    nb = seq // BLOCK
    row = lambda b, s: b * nb + s
    kcol = (2 * CONV_CH + ATTN_W + 2 * D_MODEL) // KV_W
    return pl.pallas_call(
        _attn_body,
        grid=(batch, nb),
        in_specs=[pl.BlockSpec(memory_space=pltpu.SMEM),
                  pl.BlockSpec((BLOCK, 1), lambda b, s: (row(b, s), 0)),
                  _resident((1, LANES)),
                  pl.BlockSpec((BLOCK, ATTN_W), lambda b, s: (row(b, s), 2)),
                  pl.BlockSpec((BLOCK, KV_W), lambda b, s: (row(b, s), kcol)),
                  pl.BlockSpec((BLOCK, KV_W), lambda b, s: (row(b, s), kcol + 1)),
                  pl.BlockSpec((BLOCK, D_MODEL), lambda b, s: (row(b, s), 4)),
                  _resident((ATTN_W, D_MODEL))],
        out_specs=pl.BlockSpec((BLOCK, D_MODEL), lambda b, s: (row(b, s), 0)),
        out_shape=jax.ShapeDtypeStruct((n, D_MODEL), jnp.float32),
        scratch_shapes=[pltpu.VMEM((BLOCK, KV_W), jnp.float32),
                        pltpu.VMEM((BLOCK, KV_W), jnp.float32)],
        compiler_params=_cparams(("arbitrary", "arbitrary")),
        name="attn_branch",
    )(sinks, pos_col, jnp.asarray(_rope_inv_freq()), z, z, z, z, w_attn_o)


def _topk_rows(vals, k):
    neg_rows = -lax.broadcasted_iota(jnp.int32, vals.shape, 0).astype(jnp.float32)
    top_v, top_i = [], []
    for _ in range(k):
        m = jnp.max(vals, axis=0, keepdims=True)
        neg_idx = jnp.max(jnp.where(vals == m, neg_rows, -jnp.inf), axis=0, keepdims=True)
        top_v.append(m)
        top_i.append(neg_idx)
        vals = jnp.where(neg_rows == neg_idx, -jnp.inf, vals)
    return jnp.concatenate(top_v, axis=0), (-jnp.concatenate(top_i, axis=0)).astype(jnp.int32)


def _take_rows(table, idx):
    rows = lax.broadcasted_iota(jnp.int32, table.shape, 0)
    return jnp.sum(jnp.where(rows == idx, table, 0), axis=0, keepdims=True)


def _pair_candidates(s1, s2):
    k = PEER_TOPK
    t = s1.shape[1]
    r8 = lax.broadcasted_iota(jnp.int32, (SUBLANES, t), 0)
    r16 = lax.broadcasted_iota(jnp.int32, (k, t), 0)
    vals = [s1[0:1, :] + s2, s1[1:2, :] + s2[0:SUBLANES, :], s1[SUBLANES:k, :] + s2[0:1, :]]
    flat = [r16, k + r8, (r8 + SUBLANES) * k]
    for b in range(k // 3):
        keep = (r8 >= 2) & (r8 < k // (b + 1))
        vals.append(jnp.where(keep, s1[0:SUBLANES, :] + s2[b:b + 1, :], -jnp.inf))
        flat.append(jnp.where(keep, r8 * k + b, k * k))
    return jnp.concatenate(vals, axis=0), jnp.concatenate(flat, axis=0)


def _topk_flat(vals, flat, k):
    neg_flat = -flat.astype(jnp.float32)
    top_v, top_p = [], []
    for _ in range(k):
        m = jnp.max(vals, axis=0, keepdims=True)
        neg_p = jnp.max(jnp.where(vals == m, neg_flat, -jnp.inf), axis=0, keepdims=True)
        top_v.append(m)
        top_p.append(neg_p)
        vals = jnp.where(neg_flat == neg_p, -jnp.inf, vals)
    return jnp.concatenate(top_v, axis=0), (-jnp.concatenate(top_p, axis=0)).astype(jnp.int32)


def _store_token_tiles(ref, x):
    t = x.shape[0]
    for c in range(D_MODEL // LANES):
        ref[:, c, :, :] = x[:, c * LANES:(c + 1) * LANES].reshape(t // SUBLANES, SUBLANES, LANES)


def _token_rows(t):
    start = (t >> 3) * (SUBLANES * D_MODEL // LANES) + (t & (SUBLANES - 1))
    return pl.ds(start, D_MODEL // LANES, stride=SUBLANES)


def _mix_topk_body(x_ref, cg_ref, ag_ref, wout_ref, g2_ref, wpq_ref, keys_ref,
                   x1_ref, h2_ref, roff_ref, sh_ref, gate_ref):
    merged = (cg_ref[...] + ag_ref[...]).astype(jnp.bfloat16)
    x1 = x_ref[...] + jnp.dot(merged, wout_ref[...], preferred_element_type=jnp.float32)
    h2 = x1 * lax.rsqrt(jnp.mean(x1 * x1, axis=-1, keepdims=True) + EPS) * g2_ref[...]
    _store_token_tiles(x1_ref, x1)
    _store_token_tiles(h2_ref, h2)
    qp = jnp.dot(h2.astype(jnp.bfloat16), wpq_ref[...],
                 preferred_element_type=jnp.float32).astype(jnp.bfloat16)

    gates, eidxs = [], []
    for h in range(PEER_HEADS):
        tops = []
        for c in range(2):
            gi = 2 * h + c
            q_g = qp[:, gi * PEER_DHALF:(gi + 1) * PEER_DHALF]
            sc = lax.dot_general(keys_ref[gi], q_g, (((1,), (1,)), ((), ())),
                                 preferred_element_type=jnp.float32)
            tops.append(_topk_rows(sc, PEER_TOPK))
        (s1, i1), (s2, i2) = tops
        best, pos = _topk_flat(*_pair_candidates(s1, s2), PEER_TOPK)
        e = jnp.exp(best - best[0:1, :])
        gates.append(e / jnp.sum(e, axis=0, keepdims=True))
        for j in range(PEER_TOPK):
            p = pos[j:j + 1, :]
            ia = _take_rows(i1, p >> 4)
            ib = _take_rows(i2, p & (PEER_TOPK - 1))
            eidxs.append(ia * N_KEYS + ib)
    gate_ref[...] = jnp.concatenate(gates, axis=0).T
    eidx = jnp.concatenate(eidxs, axis=0).astype(jnp.float32).T.astype(jnp.int32)
    roff_ref[...] = (eidx & (HALF_EXPERTS - 1)) * SUBLANES
    sh_ref[...] = jnp.where(eidx >= HALF_EXPERTS, 16, 0)


def _mix_topk(x2, cg, ag, w_out, g2, w_pq, keys):
    n = x2.shape[0]
    tile = lambda: pl.BlockSpec((TM_MIX, D_MODEL), lambda i: (i, 0))
    colt = lambda: pl.BlockSpec((TM_MIX, N_PICKS), lambda i: (i, 0))
    tt_shape = (n // SUBLANES, D_MODEL // LANES, SUBLANES, LANES)
    ttile = lambda: pl.BlockSpec((TM_MIX // SUBLANES,) + tt_shape[1:], lambda i: (i, 0, 0, 0))
    return pl.pallas_call(
        _mix_topk_body,
        grid=(n // TM_MIX,),
        in_specs=[tile(), tile(), tile(),
                  _resident((D_MODEL, D_MODEL)),
                  _resident((1, D_MODEL)),
                  _resident((D_MODEL, PEER_HEADS * PEER_DKEY)),
                  _resident((2 * PEER_HEADS, N_KEYS, PEER_DHALF))],
        out_specs=[ttile(), ttile(), colt(), colt(), colt()],
        out_shape=[jax.ShapeDtypeStruct(tt_shape, jnp.float32),
                   jax.ShapeDtypeStruct(tt_shape, jnp.float32),
                   jax.ShapeDtypeStruct((n, N_PICKS), jnp.int32),
                   jax.ShapeDtypeStruct((n, N_PICKS), jnp.int32),
                   jax.ShapeDtypeStruct((n, N_PICKS), jnp.float32)],
        compiler_params=_cparams(("parallel",)),
        name="mix_topk",
    )(x2, cg, ag, w_out, g2, w_pq, keys)


def _pack_table(tbl):
    bits = lax.bitcast_convert_type(tbl.astype(jnp.bfloat16), jnp.uint16).astype(jnp.uint32)
    packed = (bits[:HALF_EXPERTS] << 16) | bits[HALF_EXPERTS:]
    return lax.bitcast_convert_type(packed, jnp.int32).reshape(HALF_EXPERTS * SUBLANES, LANES)


def _expert_row(tbl_ref, roff, shift):
    w = tbl_ref[pl.ds(pl.multiple_of(roff, SUBLANES), SUBLANES), :]
    return pltpu.bitcast((w << shift) & jnp.int32(-65536), jnp.float32)


def _lane_splat_rows(rows, n_parts):
    eye = (lax.broadcasted_iota(jnp.int32, (LANES, LANES), 0)
           == lax.broadcasted_iota(jnp.int32, (LANES, LANES), 1))
    ones = jnp.ones((n_parts * LANES, LANES), jnp.bfloat16)
    blocks = []
    for rem in rows:
        pieces = []
        for _ in range(n_parts):
            piece = rem.astype(jnp.bfloat16).astype(jnp.float32)
            pieces.append(jnp.where(eye, piece, 0.0))
            rem = rem - piece
        blocks.append(jnp.concatenate(pieces, axis=1))
    lhs = jnp.concatenate(blocks, axis=0)
    return jnp.dot(lhs.astype(jnp.bfloat16), ones, preferred_element_type=jnp.float32)


def _splat_chunk(src_ref, c, dst_ref, n_parts):
    chunk = dst_ref.shape[0] // N_PICKS
    rows = [src_ref[pl.ds(c * chunk + i, 1), :].astype(jnp.float32) for i in range(chunk)]
    dst_ref[...] = _lane_splat_rows(rows, n_parts).astype(dst_ref.dtype)


def _chunked_token_loop(n_chunks, fill, tokens):
    fill(0, 0)

    def pair(i, carry):
        c = 2 * i
        fill(c + 1, 1)
        tokens(c, 0)
        fill(jnp.minimum(c + 2, n_chunks - 1), 0)
        tokens(c + 1, 1)
        return carry

    lax.fori_loop(0, n_chunks // 2, pair, 0)


def _peer_u_body(roff_ref, sh_ref, h_ref, gate_ref, tbl_ref, w_ref, s0_ref, s1_ref, a_ref):
    sub = lax.broadcasted_iota(jnp.int32, (SUBLANES, LANES), 0)
    lane = lax.broadcasted_iota(jnp.int32, (SUBLANES, LANES), 1)
    low = sub < SUBLANES // 2
    slot = lane - jnp.where(low, 0, N_PICKS // 2)
    npair = N_PICKS // 2
    s_refs = (s0_ref, s1_ref)
    chunk = CHUNK_PEER_U

    def fill(c, b):
        _splat_chunk(sh_ref, c, s_refs[b], 1)

    def tokens(c, b):
        s_ref = s_refs[b]
        for i in range(chunk):
            t = c * chunk + i
            hv = h_ref[_token_rows(t), :]
            z = jnp.zeros((SUBLANES, LANES), jnp.float32)
            for j in range(npair):
                ka, kb = i * N_PICKS + j, i * N_PICKS + j + npair
                pa = _expert_row(tbl_ref, roff_ref[t, j], s_ref[ka:ka + 1, :]) * hv
                pb = _expert_row(tbl_ref, roff_ref[t, j + npair], s_ref[kb:kb + 1, :]) * hv
                m = jnp.where(low, pa, pb) + pltpu.roll(jnp.where(low, pb, pa), SUBLANES // 2, 0)
                z = jnp.where(slot == j, jnp.sum(m, axis=1, keepdims=True), z)
            a_ref[pl.ds(t, 1), :] = jnp.sum(z, axis=0, keepdims=True)

    _chunked_token_loop(roff_ref.shape[0] // chunk, fill, tokens)
    a = a_ref[...]
    w_ref[...] = gate_ref[...] * (0.5 * a * (1.0 + lax.erf(a * (2.0 ** -0.5))))


def _peer_specs():
    picks_smem = pl.BlockSpec((TB_PEER, N_PICKS), lambda i: (i, 0), memory_space=pltpu.SMEM)
    picks_vmem = pl.BlockSpec((TB_PEER, N_PICKS), lambda i: (i, 0))
    token_tiles = pl.BlockSpec((TB_PEER * SUBLANES, LANES), lambda i: (i, 0))
    return picks_smem, picks_vmem, token_tiles


def _splat_scratch(chunk, dtype):
    return pltpu.VMEM((chunk * N_PICKS, LANES), dtype)


def _peer_u(roff, sh, h2v, gate, tbl):
    n = roff.shape[0]
    picks_smem, picks_vmem, token_tiles = _peer_specs()
    return pl.pallas_call(
        _peer_u_body,
        grid=(n // TB_PEER,),
        in_specs=[picks_smem, picks_vmem, token_tiles, picks_vmem, _resident(tbl.shape)],
        out_specs=picks_vmem,
        out_shape=jax.ShapeDtypeStruct((n, N_PICKS), jnp.float32),
        scratch_shapes=[_splat_scratch(CHUNK_PEER_U, jnp.int32), _splat_scratch(CHUNK_PEER_U, jnp.int32),
                        pltpu.VMEM((TB_PEER, N_PICKS), jnp.float32)],
        compiler_params=_cparams(("arbitrary",)),
        name="peer_u",
    )(roff, sh, h2v, gate, tbl)


def _peer_v_body(final_norm, roff_ref, sh_ref, w_ref, x1_ref, fg_ref, tbl_ref, out_ref,
                 s0_ref, s1_ref, w0_ref, w1_ref, y_ref):
    s_refs = (s0_ref, s1_ref)
    w_refs = (w0_ref, w1_ref)
    chunk = CHUNK_PEER_V

    def fill(c, b):
        _splat_chunk(sh_ref, c, s_refs[b], 1)
        _splat_chunk(w_ref, c, w_refs[b], 2)

    def tokens(c, b):
        s_ref, ws_ref = s_refs[b], w_refs[b]
        for i in range(chunk):
            t = c * chunk + i
            acc = [jnp.zeros((SUBLANES, LANES), jnp.float32) for _ in range(2)]
            for k in range(N_PICKS):
                r = i * N_PICKS + k
                row = _expert_row(tbl_ref, roff_ref[t, k], s_ref[r:r + 1, :])
                acc[k % 2] = acc[k % 2] + ws_ref[r:r + 1, :] * row
            y_ref[_token_rows(t), :] = x1_ref[_token_rows(t), :] + (acc[0] + acc[1])

    _chunked_token_loop(roff_ref.shape[0] // chunk, fill, tokens)

    tb = roff_ref.shape[0]
    y4 = y_ref[...].reshape(tb // SUBLANES, D_MODEL // LANES, SUBLANES, LANES)
    y = jnp.concatenate([y4[:, c].reshape(tb, LANES) for c in range(D_MODEL // LANES)], axis=1)
    if final_norm:
        y = y * lax.rsqrt(jnp.mean(y * y, axis=-1, keepdims=True) + EPS) * fg_ref[...]
    out_ref[...] = y


def _peer_v(roff, sh, w, x1v, final_g, tbl, final_norm):
    n = roff.shape[0]
    picks_smem, picks_vmem, token_tiles = _peer_specs()
    return pl.pallas_call(
        functools.partial(_peer_v_body, final_norm),
        grid=(n // TB_PEER,),
        in_specs=[picks_smem, picks_vmem, picks_vmem, token_tiles, _resident((1, D_MODEL)), _resident(tbl.shape)],
        out_specs=pl.BlockSpec((TB_PEER, D_MODEL), lambda i: (i, 0)),
        out_shape=jax.ShapeDtypeStruct((n, D_MODEL), jnp.float32),
        scratch_shapes=[_splat_scratch(CHUNK_PEER_V, jnp.int32), _splat_scratch(CHUNK_PEER_V, jnp.int32),
                        _splat_scratch(CHUNK_PEER_V, jnp.float32), _splat_scratch(CHUNK_PEER_V, jnp.float32),
                        pltpu.VMEM((TB_PEER * SUBLANES, LANES), jnp.float32)],
        compiler_params=_cparams(("arbitrary",)),
        name="peer_v",
    )(roff, sh, w, x1v, final_g, tbl)


def _reorder_in_cols(a):
    c0 = CONV_CH
    c1 = c0 + CONV_CH
    c2 = c1 + ATTN_W
    c3 = c2 + KV_W
    c4 = c3 + KV_W
    c5 = c4 + D_MODEL
    return jnp.concatenate([a[..., :c2], a[..., c4:c5], a[..., c5:], a[..., c2:c3], a[..., c3:c4]], axis=-1)


def kernel(x, positions, norm1_g, w_in, b_in, conv_w, conv_b, conv_ln_g, conv_ln_b, w_conv_out,
           attn_sinks, w_attn_o, w_out, norm2_g, w_peer_q, peer_sub_keys, peer_u, peer_v, final_g):
    batch, seq, d = x.shape
    n = batch * seq
    depth = w_in.shape[0]
    bf = jnp.bfloat16
    row = lambda a: a.reshape(1, -1)

    x2 = x.reshape(n, d)
    pos_col = positions.reshape(n, 1)
    for l in range(depth):
        z = _in_proj(x2, row(norm1_g[l]), _reorder_in_cols(w_in[l]).astype(bf), row(_reorder_in_cols(b_in[l])))
        cg = _conv_branch(z, batch, seq, conv_w[l], row(conv_b[l]), row(conv_ln_g[l]), row(conv_ln_b[l]),
                          w_conv_out[l].astype(bf))
        ag = _attn_branch(z, pos_col, batch, seq, attn_sinks[l], w_attn_o[l].astype(bf))
        keys = peer_sub_keys[l].reshape(2 * PEER_HEADS, N_KEYS, PEER_DHALF).astype(bf)
        x1, h2, roff, sh, gate = _mix_topk(x2, cg, ag, w_out[l].astype(bf), row(norm2_g[l]),
                                           w_peer_q[l].astype(bf), keys)
        w = _peer_u(roff, sh, h2.reshape(n * SUBLANES, LANES), gate, _pack_table(peer_u[l]))
        x2 = _peer_v(roff, sh, w, x1.reshape(n * SUBLANES, LANES), row(final_g), _pack_table(peer_v[l]),
                     final_norm=(l == depth - 1))
    return x2.reshape(batch, seq, d)
```

```python
import functools

import jax
import jax.numpy as jnp
import numpy as np
from jax import lax
from jax.experimental import pallas as pl
from jax.experimental.pallas import tpu as pltpu

D_MODEL = 1024
CONV_CH = D_MODEL
CONV_WIDTH = 31
HEAD_DIM = 64
N_Q_HEADS = 16
N_KV_HEADS = 2
GROUP = N_Q_HEADS // N_KV_HEADS
ATTN_W = N_Q_HEADS * HEAD_DIM
KV_W = N_KV_HEADS * HEAD_DIM
WINDOW = 128
BLOCK = 128
ROPE_THETA = 10000.0
PEER_HEADS = 8
N_KEYS = 128
N_EXPERTS = N_KEYS * N_KEYS
PEER_DKEY = 256
PEER_DHALF = PEER_DKEY // 2
PEER_TOPK = 16
N_PICKS = PEER_HEADS * PEER_TOPK
EPS = 1e-6
NEG = -1e30

SUBLANES = 8
LANES = 128
HALF_EXPERTS = N_EXPERTS // 2
V7X_VMEM_BYTES = 64 * 1024 * 1024
VMEM_LIMIT = V7X_VMEM_BYTES * 7 // 8

Z_COLS = 2 * CONV_CH + ATTN_W + 2 * D_MODEL + 2 * KV_W

TM_PROJ = 256
T_CONV = 512
HALO = 32
CONV_ROWS = 64
TM_MIX = 256
TB_PEER = 256
CHUNK_PEER_U = 8
CHUNK_PEER_V = 2
ROFF_SPLIT_U = 4
ROFF_SPLIT_V = 1


def _cparams(sem):
    return pltpu.CompilerParams(dimension_semantics=sem, vmem_limit_bytes=VMEM_LIMIT)


def _resident(shape):
    nd = len(shape)
    return pl.BlockSpec(shape, lambda *_: (0,) * nd, pipeline_mode=pl.Buffered(1))


def _in_proj_body(x_ref, g_ref, w_ref, b_ref, z_ref):
    x = x_ref[...]
    h = x * lax.rsqrt(jnp.mean(x * x, axis=-1, keepdims=True) + EPS) * g_ref[...]
    z_ref[...] = jnp.dot(h.astype(jnp.bfloat16), w_ref[...],
                         preferred_element_type=jnp.float32) + b_ref[...]


def _in_proj(x2, g1, w_in, b_in):
    n = x2.shape[0]
    return pl.pallas_call(
        _in_proj_body,
        grid=(n // TM_PROJ,),
        in_specs=[pl.BlockSpec((TM_PROJ, D_MODEL), lambda i: (i, 0)),
                  _resident((1, D_MODEL)),
                  _resident((D_MODEL, Z_COLS)),
                  _resident((1, Z_COLS))],
        out_specs=pl.BlockSpec((TM_PROJ, Z_COLS), lambda i: (i, 0)),
        out_shape=jax.ShapeDtypeStruct((n, Z_COLS), jnp.float32),
        compiler_params=_cparams(("parallel",)),
        name="in_proj",
    )(x2, g1, w_in, b_in)


def _conv_body(val_ref, gate_ref, gc_ref, cw_ref, cb_ref, lg_ref, lb_ref, wo_ref, out_ref, ext_ref, y_ref):
    s = pl.program_id(1)
    t = val_ref.shape[0]
    n_chunks = CONV_CH // LANES

    @pl.when(s == 0)
    def _():
        ext_ref[:, 0:HALO, :] = jnp.zeros((n_chunks, HALO, LANES), jnp.float32)

    @pl.when(s != 0)
    def _():
        ext_ref[:, 0:HALO, :] = ext_ref[:, t:t + HALO, :]

    for c in range(n_chunks):
        lanes = slice(c * LANES, (c + 1) * LANES)
        ext_ref[c, HALO:HALO + t, :] = val_ref[:, lanes] * jax.nn.sigmoid(gate_ref[:, lanes])

    base = HALO - (CONV_WIDTH - 1)
    for c in range(n_chunks):
        lanes = slice(c * LANES, (c + 1) * LANES)
        for r0 in range(0, t, CONV_ROWS):
            acc = jnp.zeros((CONV_ROWS, LANES), jnp.float32) + cb_ref[:, lanes]
            for j in range(CONV_WIDTH):
                acc = acc + cw_ref[j:j + 1, lanes] * ext_ref[c, base + j + r0:base + j + r0 + CONV_ROWS, :]
            y_ref[r0:r0 + CONV_ROWS, lanes] = acc

    y = y_ref[...]
    mu = jnp.mean(y, axis=-1, keepdims=True)
    yc = y - mu
    yn = yc * lax.rsqrt(jnp.mean(yc * yc, axis=-1, keepdims=True) + EPS)
    yn = yn * lg_ref[...] + lb_ref[...]
    act = yn * jax.nn.sigmoid(yn)
    co = jnp.dot(act.astype(jnp.bfloat16), wo_ref[...], preferred_element_type=jnp.float32)
    out_ref[...] = jax.nn.sigmoid(gc_ref[...]) * co


def _conv_branch(z, batch, seq, conv_w, conv_b, ln_g, ln_b, w_conv_out):
    n = z.shape[0]
    nt = seq // T_CONV
    row = lambda b, s: b * nt + s
    return pl.pallas_call(
        _conv_body,
        grid=(batch, nt),
        in_specs=[pl.BlockSpec((T_CONV, CONV_CH), lambda b, s: (row(b, s), 0)),
                  pl.BlockSpec((T_CONV, CONV_CH), lambda b, s: (row(b, s), 1)),
                  pl.BlockSpec((T_CONV, D_MODEL), lambda b, s: (row(b, s), 3)),
                  _resident((CONV_WIDTH, CONV_CH)),
                  _resident((1, CONV_CH)),
                  _resident((1, CONV_CH)),
                  _resident((1, CONV_CH)),
                  _resident((CONV_CH, D_MODEL))],
        out_specs=pl.BlockSpec((T_CONV, D_MODEL), lambda b, s: (row(b, s), 0)),
        out_shape=jax.ShapeDtypeStruct((n, D_MODEL), jnp.float32),
        scratch_shapes=[pltpu.VMEM((CONV_CH // LANES, T_CONV + HALO, LANES), jnp.float32),
                        pltpu.VMEM((T_CONV, CONV_CH), jnp.float32)],
        compiler_params=_cparams(("arbitrary", "arbitrary")),
        name="conv_branch",
    )(z, z, z, conv_w, conv_b, ln_g, ln_b, w_conv_out)


def _rot_half(x, first_half):
    return jnp.where(first_half, pltpu.roll(x, LANES - HEAD_DIM // 2, 1), pltpu.roll(x, HEAD_DIM // 2, 1))


def _rope_inv_freq():
    half = HEAD_DIM // 2
    f = (np.arange(LANES) % half).astype(np.float32)
    return (np.float32(ROPE_THETA) ** (-(f * np.float32(2.0) / np.float32(HEAD_DIM)))).astype(np.float32)[None, :]


def _attn_body(sinks_ref, pos_ref, inv_ref, q_ref, k_ref, v_ref, ga_ref, wo_ref, out_ref, kp_ref, vp_ref):
    n = pl.program_id(1)
    half = HEAD_DIM // 2

    lane = lax.broadcasted_iota(jnp.int32, (1, LANES), 1)
    ang = pos_ref[...].astype(jnp.float32) * inv_ref[...]
    cos = jnp.cos(ang)
    first_half = (lane % HEAD_DIM) < half
    sin = jnp.where(first_half, -jnp.sin(ang), jnp.sin(ang))

    def rope(x):
        return x * cos + _rot_half(x, first_half) * sin

    k_cur = rope(k_ref[...])
    v_cur = v_ref[...]

    @pl.when(n == 0)
    def _():
        kp_ref[...] = jnp.zeros_like(kp_ref)
        vp_ref[...] = jnp.zeros_like(vp_ref)

    kcat = jnp.concatenate([kp_ref[...], k_cur], axis=0)
    vcat = jnp.concatenate([vp_ref[...], v_cur], axis=0)
    kp_ref[...] = k_cur
    vp_ref[...] = v_cur

    lo = lane < HEAD_DIM
    kvar, vvar = {}, {}
    for g in range(N_KV_HEADS):
        own = lo if g == 0 else jnp.logical_not(lo)
        k_own = jnp.where(own, kcat, 0.0)
        v_own = jnp.where(own, vcat, 0.0)
        k_sw = pltpu.roll(k_own, HEAD_DIM, 1)
        v_sw = pltpu.roll(v_own, HEAD_DIM, 1)
        kvar[g] = ((k_own, k_sw) if g == 0 else (k_sw, k_own))
        vvar[g] = ((v_own, v_sw) if g == 0 else (v_sw, v_own))
        kvar[g] = tuple(a.astype(jnp.bfloat16) for a in kvar[g])
        vvar[g] = tuple(a.astype(jnp.bfloat16) for a in vvar[g])

    qi = lax.broadcasted_iota(jnp.int32, (BLOCK, 2 * BLOCK), 0)
    kj = lax.broadcasted_iota(jnp.int32, (BLOCK, 2 * BLOCK), 1)
    rel = qi + BLOCK - kj
    mask = (rel >= 0) & (rel < WINDOW) & ((kj >= BLOCK) | (n > 0))

    chunks = []
    for c in range(ATTN_W // LANES):
        q_c = rope(q_ref[:, c * LANES:(c + 1) * LANES]).astype(jnp.bfloat16)
        o_c = jnp.zeros((BLOCK, LANES), jnp.float32)
        for pos in range(2):
            h = 2 * c + pos
            g = h // GROUP
            s = lax.dot_general(q_c, kvar[g][pos], (((1,), (1,)), ((), ())),
                                preferred_element_type=jnp.float32) * (HEAD_DIM ** -0.5)
            s = jnp.where(mask, s, NEG)
            sk = sinks_ref[h]
            m = jnp.maximum(jnp.max(s, axis=-1, keepdims=True), sk)
            e = jnp.exp(s - m)
            p = e / (jnp.sum(e, axis=-1, keepdims=True) + jnp.exp(sk - m))
            o_c = o_c + jnp.dot(p.astype(jnp.bfloat16), vvar[g][pos],
                                preferred_element_type=jnp.float32)
        chunks.append(o_c)
    attn = jnp.concatenate(chunks, axis=1).astype(jnp.bfloat16)
    ao = jnp.dot(attn, wo_ref[...], preferred_element_type=jnp.float32)
    out_ref[...] = jax.nn.sigmoid(ga_ref[...]) * ao


def _attn_branch(z, pos_col, batch, seq, sinks, w_attn_o):
    n = z.shape[0]
    nb = seq // BLOCK
    row = lambda b, s: b * nb + s
    kcol = (2 * CONV_CH + ATTN_W + 2 * D_MODEL) // KV_W
    return pl.pallas_call(
        _attn_body,
        grid=(batch, nb),
        in_specs=[pl.BlockSpec(memory_space=pltpu.SMEM),
                  pl.BlockSpec((BLOCK, 1), lambda b, s: (row(b, s), 0)),
                  _resident((1, LANES)),
                  pl.BlockSpec((BLOCK, ATTN_W), lambda b, s: (row(b, s), 2)),
                  pl.BlockSpec((BLOCK, KV_W), lambda b, s: (row(b, s), kcol)),
                  pl.BlockSpec((BLOCK, KV_W), lambda b, s: (row(b, s), kcol + 1)),
                  pl.BlockSpec((BLOCK, D_MODEL), lambda b, s: (row(b, s), 4)),
                  _resident((ATTN_W, D_MODEL))],
        out_specs=pl.BlockSpec((BLOCK, D_MODEL), lambda b, s: (row(b, s), 0)),
        out_shape=jax.ShapeDtypeStruct((n, D_MODEL), jnp.float32),
        scratch_shapes=[pltpu.VMEM((BLOCK, KV_W), jnp.float32),
                        pltpu.VMEM((BLOCK, KV_W), jnp.float32)],
        compiler_params=_cparams(("arbitrary", "arbitrary")),
        name="attn_branch",
    )(sinks, pos_col, jnp.asarray(_rope_inv_freq()), z, z, z, z, w_attn_o)


def _topk_rows(vals, k):
    neg_rows = -lax.broadcasted_iota(jnp.int32, vals.shape, 0).astype(jnp.float32)
    top_v, top_i = [], []
    for _ in range(k):
        m = jnp.max(vals, axis=0, keepdims=True)
        neg_idx = jnp.max(jnp.where(vals == m, neg_rows, -jnp.inf), axis=0, keepdims=True)
        top_v.append(m)
        top_i.append(neg_idx)
        vals = jnp.where(neg_rows == neg_idx, -jnp.inf, vals)
    return jnp.concatenate(top_v, axis=0), (-jnp.concatenate(top_i, axis=0)).astype(jnp.int32)


def _take_rows(table, idx):
    rows = lax.broadcasted_iota(jnp.int32, table.shape, 0)
    return jnp.sum(jnp.where(rows == idx, table, 0), axis=0, keepdims=True)


def _pair_candidates(s1, s2):
    k = PEER_TOPK
    t = s1.shape[1]
    r8 = lax.broadcasted_iota(jnp.int32, (SUBLANES, t), 0)
    r16 = lax.broadcasted_iota(jnp.int32, (k, t), 0)
    vals = [s1[0:1, :] + s2, s1[1:2, :] + s2[0:SUBLANES, :], s1[SUBLANES:k, :] + s2[0:1, :]]
    flat = [r16, k + r8, (r8 + SUBLANES) * k]
    for b in range(k // 3):
        keep = (r8 >= 2) & (r8 < k // (b + 1))
        vals.append(jnp.where(keep, s1[0:SUBLANES, :] + s2[b:b + 1, :], -jnp.inf))
        flat.append(jnp.where(keep, r8 * k + b, k * k))
    return jnp.concatenate(vals, axis=0), jnp.concatenate(flat, axis=0)


def _topk_flat(vals, flat, k):
    neg_flat = -flat.astype(jnp.float32)
    top_v, top_p = [], []
    for _ in range(k):
        m = jnp.max(vals, axis=0, keepdims=True)
        neg_p = jnp.max(jnp.where(vals == m, neg_flat, -jnp.inf), axis=0, keepdims=True)
        top_v.append(m)
        top_p.append(neg_p)
        vals = jnp.where(neg_flat == neg_p, -jnp.inf, vals)
    return jnp.concatenate(top_v, axis=0), (-jnp.concatenate(top_p, axis=0)).astype(jnp.int32)


def _store_token_tiles(ref, x):
    t = x.shape[0]
    for c in range(D_MODEL // LANES):
        ref[:, c, :, :] = x[:, c * LANES:(c + 1) * LANES].reshape(t // SUBLANES, SUBLANES, LANES)


def _token_rows(t):
    start = (t >> 3) * (SUBLANES * D_MODEL // LANES) + (t & (SUBLANES - 1))
    return pl.ds(start, D_MODEL // LANES, stride=SUBLANES)


def _mix_topk_body(x_ref, cg_ref, ag_ref, wout_ref, g2_ref, wpq_ref, keys_ref,
                   x1_ref, h2_ref, roff_ref, sh_ref, gate_ref):
    merged = (cg_ref[...] + ag_ref[...]).astype(jnp.bfloat16)
    x1 = x_ref[...] + jnp.dot(merged, wout_ref[...], preferred_element_type=jnp.float32)
    h2 = x1 * lax.rsqrt(jnp.mean(x1 * x1, axis=-1, keepdims=True) + EPS) * g2_ref[...]
    _store_token_tiles(x1_ref, x1)
    _store_token_tiles(h2_ref, h2)
    qp = jnp.dot(h2.astype(jnp.bfloat16), wpq_ref[...],
                 preferred_element_type=jnp.float32).astype(jnp.bfloat16)

    gates, eidxs = [], []
    for h in range(PEER_HEADS):
        tops = []
        for c in range(2):
            gi = 2 * h + c
            q_g = qp[:, gi * PEER_DHALF:(gi + 1) * PEER_DHALF]
            sc = lax.dot_general(keys_ref[gi], q_g, (((1,), (1,)), ((), ())),
                                 preferred_element_type=jnp.float32)
            tops.append(_topk_rows(sc, PEER_TOPK))
        (s1, i1), (s2, i2) = tops
        best, pos = _topk_flat(*_pair_candidates(s1, s2), PEER_TOPK)
        e = jnp.exp(best - best[0:1, :])
        gates.append(e / jnp.sum(e, axis=0, keepdims=True))
        for j in range(PEER_TOPK):
            p = pos[j:j + 1, :]
            ia = _take_rows(i1, p >> 4)
            ib = _take_rows(i2, p & (PEER_TOPK - 1))
            eidxs.append(ia * N_KEYS + ib)
    gate_ref[...] = jnp.concatenate(gates, axis=0).T
    eidx = jnp.concatenate(eidxs, axis=0).astype(jnp.float32).T.astype(jnp.int32)
    roff_ref[...] = (eidx & (HALF_EXPERTS - 1)) * SUBLANES
    sh_ref[...] = jnp.where(eidx >= HALF_EXPERTS, 16, 0)


def _mix_topk(x2, cg, ag, w_out, g2, w_pq, keys):
    n = x2.shape[0]
    tile = lambda: pl.BlockSpec((TM_MIX, D_MODEL), lambda i: (i, 0))
    colt = lambda: pl.BlockSpec((TM_MIX, N_PICKS), lambda i: (i, 0))
    tt_shape = (n // SUBLANES, D_MODEL // LANES, SUBLANES, LANES)
    ttile = lambda: pl.BlockSpec((TM_MIX // SUBLANES,) + tt_shape[1:], lambda i: (i, 0, 0, 0))
    return pl.pallas_call(
        _mix_topk_body,
        grid=(n // TM_MIX,),
        in_specs=[tile(), tile(), tile(),
                  _resident((D_MODEL, D_MODEL)),
                  _resident((1, D_MODEL)),
                  _resident((D_MODEL, PEER_HEADS * PEER_DKEY)),
                  _resident((2 * PEER_HEADS, N_KEYS, PEER_DHALF))],
        out_specs=[ttile(), ttile(), colt(), colt(), colt()],
        out_shape=[jax.ShapeDtypeStruct(tt_shape, jnp.float32),
                   jax.ShapeDtypeStruct(tt_shape, jnp.float32),
                   jax.ShapeDtypeStruct((n, N_PICKS), jnp.int32),
                   jax.ShapeDtypeStruct((n, N_PICKS), jnp.int32),
                   jax.ShapeDtypeStruct((n, N_PICKS), jnp.float32)],
        compiler_params=_cparams(("parallel",)),
        name="mix_topk",
    )(x2, cg, ag, w_out, g2, w_pq, keys)


def _pack_table(tbl):
    bits = lax.bitcast_convert_type(tbl.astype(jnp.bfloat16), jnp.uint16).astype(jnp.uint32)
    packed = (bits[:HALF_EXPERTS] << 16) | bits[HALF_EXPERTS:]
    return lax.bitcast_convert_type(packed, jnp.int32).reshape(HALF_EXPERTS * SUBLANES, LANES)


def _expert_row(tbl_ref, roff, shift):
    w = tbl_ref[pl.ds(pl.multiple_of(roff, SUBLANES), SUBLANES), :]
    return pltpu.bitcast((w << shift) & jnp.int32(-65536), jnp.float32)


def _lane_splat_rows(rows, n_parts):
    eye = (lax.broadcasted_iota(jnp.int32, (LANES, LANES), 0)
           == lax.broadcasted_iota(jnp.int32, (LANES, LANES), 1))
    ones = jnp.ones((n_parts * LANES, LANES), jnp.bfloat16)
    blocks = []
    for rem in rows:
        pieces = []
        for _ in range(n_parts):
            piece = rem.astype(jnp.bfloat16).astype(jnp.float32)
            pieces.append(jnp.where(eye, piece, 0.0))
            rem = rem - piece
        blocks.append(jnp.concatenate(pieces, axis=1))
    lhs = jnp.concatenate(blocks, axis=0)
    return jnp.dot(lhs.astype(jnp.bfloat16), ones, preferred_element_type=jnp.float32)


def _splat_chunk(src_ref, c, dst_ref, n_parts):
    chunk = dst_ref.shape[0] // N_PICKS
    rows = [src_ref[pl.ds(c * chunk + i, 1), :].astype(jnp.float32) for i in range(chunk)]
    dst_ref[...] = _lane_splat_rows(rows, n_parts).astype(dst_ref.dtype)


def _chunked_token_loop(n_chunks, fill, tokens):
    fill(0, 0)

    def pair(i, carry):
        c = 2 * i
        fill(c + 1, 1)
        tokens(c, 0)
        fill(jnp.minimum(c + 2, n_chunks - 1), 0)
        tokens(c + 1, 1)
        return carry

    lax.fori_loop(0, n_chunks // 2, pair, 0)


def _row_offset(roff_refs, t, k):
    width = N_PICKS // len(roff_refs)
    return roff_refs[k // width][t * width + k % width]


def _pick_order(split):
    width = N_PICKS // split
    return [q * width + m for m in range(width) for q in range(split)]


def _split_row_offsets(roff, split):
    width = N_PICKS // split
    return [roff[:, q * width:(q + 1) * width].reshape(-1) for q in range(split)]


def _peer_u_body(*refs):
    roff_refs = refs[:ROFF_SPLIT_U]
    sh_ref, h_ref, gate_ref, tbl_ref, w_ref, s0_ref, s1_ref, z_ref = refs[ROFF_SPLIT_U:]
    tb = sh_ref.shape[0]
    sub = lax.broadcasted_iota(jnp.int32, (SUBLANES, LANES), 0)
    low = sub < SUBLANES // 2
    npair = N_PICKS // 2
    s_refs = (s0_ref, s1_ref)
    chunk = CHUNK_PEER_U

    def fill(c, b):
        _splat_chunk(sh_ref, c, s_refs[b], 1)

    z_ref[...] = jnp.zeros_like(z_ref)

    def tokens(c, b):
        s_ref = s_refs[b]
        for i in range(chunk):
            t = c * chunk + i
            hv = h_ref[_token_rows(t), :]
            tile = pl.ds(pl.multiple_of(t * SUBLANES, SUBLANES), SUBLANES)
            for j in (k for k in _pick_order(ROFF_SPLIT_U) if k < npair):
                ka, kb = i * N_PICKS + j, i * N_PICKS + j + npair
                pa = _expert_row(tbl_ref, _row_offset(roff_refs, t, j), s_ref[ka:ka + 1, :]) * hv
                pb = _expert_row(tbl_ref, _row_offset(roff_refs, t, j + npair), s_ref[kb:kb + 1, :]) * hv
                m = jnp.where(low, pa, pb) + pltpu.roll(jnp.where(low, pb, pa), SUBLANES // 2, 0)
                z_ref[tile, j:j + 1] = jnp.sum(m, axis=1, keepdims=True)

    _chunked_token_loop(tb // chunk, fill, tokens)

    z = z_ref[...]
    zsub = lax.broadcasted_iota(jnp.int32, z.shape, 0) % SUBLANES
    z_lo = jnp.where(zsub < SUBLANES // 2, z, 0.0).reshape(tb, SUBLANES, LANES).sum(axis=1)
    z_hi = jnp.where(zsub >= SUBLANES // 2, z, 0.0).reshape(tb, SUBLANES, LANES).sum(axis=1)
    lane = lax.broadcasted_iota(jnp.int32, (tb, LANES), 1)
    a = jnp.where(lane < npair, z_lo, pltpu.roll(z_hi, npair, 1))
    w_ref[...] = gate_ref[...] * (0.5 * a * (1.0 + lax.erf(a * (2.0 ** -0.5))))


def _peer_specs(split):
    picks_smem = [pl.BlockSpec((TB_PEER * N_PICKS // split,), lambda i: (i,), memory_space=pltpu.SMEM)
                  for _ in range(split)]
    picks_vmem = pl.BlockSpec((TB_PEER, N_PICKS), lambda i: (i, 0))
    token_tiles = pl.BlockSpec((TB_PEER * SUBLANES, LANES), lambda i: (i, 0))
    return picks_smem, picks_vmem, token_tiles


def _splat_scratch(chunk, dtype):
    return pltpu.VMEM((chunk * N_PICKS, LANES), dtype)


def _peer_u(roff, sh, h2v, gate, tbl):
    n = roff.shape[0]
    picks_smem, picks_vmem, token_tiles = _peer_specs(ROFF_SPLIT_U)
    return pl.pallas_call(
        _peer_u_body,
        grid=(n // TB_PEER,),
        in_specs=picks_smem + [picks_vmem, token_tiles, picks_vmem, _resident(tbl.shape)],
        out_specs=picks_vmem,
        out_shape=jax.ShapeDtypeStruct((n, N_PICKS), jnp.float32),
        scratch_shapes=[_splat_scratch(CHUNK_PEER_U, jnp.int32), _splat_scratch(CHUNK_PEER_U, jnp.int32),
                        pltpu.VMEM((TB_PEER * SUBLANES, LANES), jnp.float32)],
        compiler_params=_cparams(("arbitrary",)),
        name="peer_u",
    )(*_split_row_offsets(roff, ROFF_SPLIT_U), sh, h2v, gate, tbl)


def _peer_v_body(final_norm, *refs):
    roff_refs = refs[:ROFF_SPLIT_V]
    sh_ref, w_ref, x1_ref, fg_ref, tbl_ref, out_ref, s0_ref, s1_ref, w0_ref, w1_ref, y_ref = refs[ROFF_SPLIT_V:]
    tb = sh_ref.shape[0]
    s_refs = (s0_ref, s1_ref)
    w_refs = (w0_ref, w1_ref)
    chunk = CHUNK_PEER_V

    def fill(c, b):
        _splat_chunk(sh_ref, c, s_refs[b], 1)
        _splat_chunk(w_ref, c, w_refs[b], 2)

    def tokens(c, b):
        s_ref, ws_ref = s_refs[b], w_refs[b]
        for i in range(chunk):
            t = c * chunk + i
            acc = [jnp.zeros((SUBLANES, LANES), jnp.float32) for _ in range(2)]
            for n_row, k in enumerate(_pick_order(ROFF_SPLIT_V)):
                r = i * N_PICKS + k
                row = _expert_row(tbl_ref, _row_offset(roff_refs, t, k), s_ref[r:r + 1, :])
                acc[n_row % 2] = acc[n_row % 2] + ws_ref[r:r + 1, :] * row
            y_ref[_token_rows(t), :] = x1_ref[_token_rows(t), :] + (acc[0] + acc[1])

    _chunked_token_loop(tb // chunk, fill, tokens)

    y4 = y_ref[...].reshape(tb // SUBLANES, D_MODEL // LANES, SUBLANES, LANES)
    y = jnp.concatenate([y4[:, c].reshape(tb, LANES) for c in range(D_MODEL // LANES)], axis=1)
    if final_norm:
        y = y * lax.rsqrt(jnp.mean(y * y, axis=-1, keepdims=True) + EPS) * fg_ref[...]
    out_ref[...] = y


def _peer_v(roff, sh, w, x1v, final_g, tbl, final_norm):
    n = roff.shape[0]
    picks_smem, picks_vmem, token_tiles = _peer_specs(ROFF_SPLIT_V)
    return pl.pallas_call(
        functools.partial(_peer_v_body, final_norm),
        grid=(n // TB_PEER,),
        in_specs=picks_smem + [picks_vmem, picks_vmem, token_tiles, _resident((1, D_MODEL)), _resident(tbl.shape)],
        out_specs=pl.BlockSpec((TB_PEER, D_MODEL), lambda i: (i, 0)),
        out_shape=jax.ShapeDtypeStruct((n, D_MODEL), jnp.float32),
        scratch_shapes=[_splat_scratch(CHUNK_PEER_V, jnp.int32), _splat_scratch(CHUNK_PEER_V, jnp.int32),
                        _splat_scratch(CHUNK_PEER_V, jnp.float32), _splat_scratch(CHUNK_PEER_V, jnp.float32),
                        pltpu.VMEM((TB_PEER * SUBLANES, LANES), jnp.float32)],
        compiler_params=_cparams(("arbitrary",)),
        name="peer_v",
    )(*_split_row_offsets(roff, ROFF_SPLIT_V), sh, w, x1v, final_g, tbl)


def _reorder_in_cols(a):
    c0 = CONV_CH
    c1 = c0 + CONV_CH
    c2 = c1 + ATTN_W
    c3 = c2 + KV_W
    c4 = c3 + KV_W
    c5 = c4 + D_MODEL
    return jnp.concatenate([a[..., :c2], a[..., c4:c5], a[..., c5:], a[..., c2:c3], a[..., c3:c4]], axis=-1)


def kernel(x, positions, norm1_g, w_in, b_in, conv_w, conv_b, conv_ln_g, conv_ln_b, w_conv_out,
           attn_sinks, w_attn_o, w_out, norm2_g, w_peer_q, peer_sub_keys, peer_u, peer_v, final_g):
    batch, seq, d = x.shape
    n = batch * seq
    depth = w_in.shape[0]
    bf = jnp.bfloat16
    row = lambda a: a.reshape(1, -1)

    x2 = x.reshape(n, d)
    pos_col = positions.reshape(n, 1)
    for l in range(depth):
        z = _in_proj(x2, row(norm1_g[l]), _reorder_in_cols(w_in[l]).astype(bf), row(_reorder_in_cols(b_in[l])))
        cg = _conv_branch(z, batch, seq, conv_w[l], row(conv_b[l]), row(conv_ln_g[l]), row(conv_ln_b[l]),
                          w_conv_out[l].astype(bf))
        ag = _attn_branch(z, pos_col, batch, seq, attn_sinks[l], w_attn_o[l].astype(bf))
        keys = peer_sub_keys[l].reshape(2 * PEER_HEADS, N_KEYS, PEER_DHALF).astype(bf)
        x1, h2, roff, sh, gate = _mix_topk(x2, cg, ag, w_out[l].astype(bf), row(norm2_g[l]),
                                           w_peer_q[l].astype(bf), keys)
        w = _peer_u(roff, sh, h2.reshape(n * SUBLANES, LANES), gate, _pack_table(peer_u[l]))
        x2 = _peer_v(roff, sh, w, x1.reshape(n * SUBLANES, LANES), row(final_g), _pack_table(peer_v[l]),
                     final_norm=(l == depth - 1))
    return x2.reshape(batch, seq, d)
```
